```python
import math
import jax, jax.numpy as jnp
from jax import lax
import numpy as np

D_MODEL = 2048
BATCH = 4
SEQ = 4096
DEPTH = 1

MLA_HEADS = 8
MLA_Q_RANK = 512
MLA_KV_RANK = 512
MLA_NOPE_DIM = 128
MLA_ROPE_DIM = 64
MLA_V_DIM = 128
MLA_QK_DIM = MLA_NOPE_DIM + MLA_ROPE_DIM
MLA_V_COLS = MLA_HEADS * MLA_V_DIM
ROPE_THETA = 10000.0

DIFF_HEADS = 4
DIFF_QK_DIM = 128
DIFF_V_DIM = 2 * DIFF_QK_DIM
DIFF_QK_COLS = DIFF_HEADS * 2 * DIFF_QK_DIM
DIFF_V_COLS = DIFF_HEADS * DIFF_V_DIM

D_FF = 5632

IN_SPLIT_SIZES = (MLA_Q_RANK, MLA_KV_RANK, MLA_ROPE_DIM, DIFF_QK_COLS, DIFF_QK_COLS, DIFF_V_COLS, D_MODEL, D_MODEL)
IN_COLS = sum(IN_SPLIT_SIZES)

BLOCK_Q = 128
LN_EPS = 1e-5
RMS_EPS = 1e-6
ALPHA = (2 * DEPTH) ** 0.25
BETA = (8 * DEPTH) ** -0.25

kernel_name = 'hybrid_mla_diffattn_macaron_deepnorm'


def _layer_norm(x, g, b):
    xf = x.astype(jnp.float32)
    mu = jnp.mean(xf, axis=-1, keepdims=True)
    var = jnp.mean(jnp.square(xf - mu), axis=-1, keepdims=True)
    return ((xf - mu) * lax.rsqrt(var + LN_EPS) * g.astype(jnp.float32) + b.astype(jnp.float32)).astype(x.dtype)


def _rms_norm(x, g):
    xf = x.astype(jnp.float32)
    ms = jnp.mean(jnp.square(xf), axis=-1, keepdims=True)
    return (xf * lax.rsqrt(ms + RMS_EPS) * g.astype(jnp.float32)).astype(x.dtype)


def _swiglu(x, w_gate, w_up, w_down):
    return (jax.nn.silu(x @ w_gate) * (x @ w_up)) @ w_down


def _rope(x, cos, sin):
    half = x.shape[-1] // 2
    xf = x.astype(jnp.float32)
    x1, x2 = xf[..., :half], xf[..., half:]
    return jnp.concatenate([x1 * cos - x2 * sin, x1 * sin + x2 * cos], axis=-1).astype(x.dtype)


def _causal_mask(q0, q1):
    return jnp.arange(q0, q1)[:, None] >= jnp.arange(q1)[None, :]


def _split_columns(proj):
    outs, start = [], 0
    for size in IN_SPLIT_SIZES:
        outs.append(proj[..., start:start + size])
        start += size
    return outs


def _mla_attention(q, k, v):
    scale = q.shape[-1] ** -0.5
    outs = []
    for i in range(q.shape[1] // BLOCK_Q):
        q0, q1 = i * BLOCK_Q, (i + 1) * BLOCK_Q
        s = jnp.einsum('bqhd,bkhd->bhqk', q[:, q0:q1], k[:, :q1]).astype(jnp.float32) * scale
        s = jnp.where(_causal_mask(q0, q1), s, -jnp.inf)
        p = jax.nn.softmax(s, axis=-1).astype(v.dtype)
        outs.append(jnp.einsum('bhqk,bkhd->bqhd', p, v[:, :q1]))
    return jnp.concatenate(outs, axis=1)


def _diff_attention(q, k, v, positions, lam):
    scale = DIFF_QK_DIM ** -0.5
    slopes = 2.0 ** (-8.0 * jnp.arange(1, DIFF_HEADS + 1, dtype=jnp.float32) / DIFF_HEADS)
    pos = positions.astype(jnp.float32)
    outs = []
    for i in range(q.shape[1] // BLOCK_Q):
        q0, q1 = i * BLOCK_Q, (i + 1) * BLOCK_Q
        s = jnp.einsum('bqhmd,bkhmd->bhmqk', q[:, q0:q1], k[:, :q1]).astype(jnp.float32) * scale
        dist = jnp.abs(pos[:, q0:q1, None] - pos[:, None, :q1])
        s = s - slopes[None, :, None, None, None] * dist[:, None, None, :, :]
        s = jnp.where(_causal_mask(q0, q1), s, -jnp.inf)
        p = jax.nn.softmax(s, axis=-1)
        a = (p[:, :, 0] - lam * p[:, :, 1]).astype(v.dtype)
        outs.append(jnp.einsum('bhqk,bkhd->bqhd', a, v[:, :q1]))
    return jnp.concatenate(outs, axis=1)


def _mixer(h, positions, w_in, q_norm_g, w_uq, kv_norm_g, w_uk, w_uv,
           lq1, lk1, lq2, lk2, subln_g, w_br_mla, w_br_diff, w_out, lambda_init):
    b, s, _ = h.shape
    c_q, c_kv, k_r, dq, dk, dv, gate_mla, gate_diff = _split_columns(h @ w_in)

    half = MLA_ROPE_DIM // 2
    inv_freq = ROPE_THETA ** (-jnp.arange(half, dtype=jnp.float32) / half)
    ang = positions.astype(jnp.float32)[:, :, None, None] * inv_freq
    cos, sin = jnp.cos(ang), jnp.sin(ang)
    q = (_rms_norm(c_q, q_norm_g) @ w_uq).reshape(b, s, MLA_HEADS, MLA_QK_DIM)
    q = jnp.concatenate([q[..., :MLA_NOPE_DIM], _rope(q[..., MLA_NOPE_DIM:], cos, sin)], axis=-1)
    c_kv = _rms_norm(c_kv, kv_norm_g)
    k_nope = (c_kv @ w_uk).reshape(b, s, MLA_HEADS, MLA_NOPE_DIM)
    v_mla = (c_kv @ w_uv).reshape(b, s, MLA_HEADS, MLA_V_DIM)
    k_rope = _rope(k_r[:, :, None, :], cos, sin)
    k = jnp.concatenate([k_nope, jnp.broadcast_to(k_rope, (b, s, MLA_HEADS, MLA_ROPE_DIM))], axis=-1)
    o_mla = _mla_attention(q, k, v_mla).reshape(b, s, MLA_V_COLS)

    f32 = jnp.float32
    lam = (jnp.exp(jnp.sum(lq1.astype(f32) * lk1.astype(f32)))
           - jnp.exp(jnp.sum(lq2.astype(f32) * lk2.astype(f32))) + lambda_init)
    o_diff = _diff_attention(dq.reshape(b, s, DIFF_HEADS, 2, DIFF_QK_DIM),
                             dk.reshape(b, s, DIFF_HEADS, 2, DIFF_QK_DIM),
                             dv.reshape(b, s, DIFF_HEADS, DIFF_V_DIM), positions, lam)
    o_diff = (_rms_norm(o_diff, subln_g) * (1.0 - lambda_init)).reshape(b, s, DIFF_V_COLS)

    y = jax.nn.sigmoid(gate_mla) * (o_mla @ w_br_mla) + jax.nn.sigmoid(gate_diff) * (o_diff @ w_br_diff)
    return y @ w_out


def setup_inputs(seed: int = 0) -> dict:
    key = jax.random.key(seed)
    ks = iter(jax.random.split(key, 40))
    f32 = jnp.float32

    def w(shape, fan_in, scale=1.0):
        return jax.random.normal(next(ks), shape, f32) * (scale * fan_in ** -0.5)

    def gain(n):
        return 1.0 + 0.02 * jax.random.normal(next(ks), (DEPTH, n), f32)

    def bias(n):
        return 0.02 * jax.random.normal(next(ks), (DEPTH, n), f32)

    x = jax.random.normal(next(ks), (BATCH, SEQ, D_MODEL), f32)
    positions = jnp.broadcast_to(jnp.arange(SEQ, dtype=jnp.int32), (BATCH, SEQ))
    col_scale = jnp.concatenate([
        jnp.ones((IN_COLS - DIFF_V_COLS - 2 * D_MODEL,), f32),
        jnp.full((DIFF_V_COLS,), BETA, f32),
        jnp.ones((2 * D_MODEL,), f32)])
    return {
        'x': x,
        'positions': positions,
        'ln1_g': gain(D_MODEL),
        'ln1_b': bias(D_MODEL),
        'ffn1_w_gate': w((DEPTH, D_MODEL, D_FF), D_MODEL, BETA),
        'ffn1_w_up': w((DEPTH, D_MODEL, D_FF), D_MODEL, BETA),
        'ffn1_w_down': w((DEPTH, D_FF, D_MODEL), D_FF, BETA),
        'w_in': w((DEPTH, D_MODEL, IN_COLS), D_MODEL) * col_scale,
        'mla_q_norm_g': gain(MLA_Q_RANK),
        'mla_w_uq': w((DEPTH, MLA_Q_RANK, MLA_HEADS * MLA_QK_DIM), MLA_Q_RANK),
        'mla_kv_norm_g': gain(MLA_KV_RANK),
        'mla_w_uk': w((DEPTH, MLA_KV_RANK, MLA_HEADS * MLA_NOPE_DIM), MLA_KV_RANK),
        'mla_w_uv': w((DEPTH, MLA_KV_RANK, MLA_V_COLS), MLA_KV_RANK, BETA),
        'diff_lambda_q1': 0.1 * jax.random.normal(next(ks), (DEPTH, DIFF_QK_DIM), f32),
        'diff_lambda_k1': 0.1 * jax.random.normal(next(ks), (DEPTH, DIFF_QK_DIM), f32),
        'diff_lambda_q2': 0.1 * jax.random.normal(next(ks), (DEPTH, DIFF_QK_DIM), f32),
        'diff_lambda_k2': 0.1 * jax.random.normal(next(ks), (DEPTH, DIFF_QK_DIM), f32),
        'diff_subln_g': gain(DIFF_V_DIM),
        'w_branch_mla': w((DEPTH, MLA_V_COLS, D_MODEL), MLA_V_COLS, BETA),
        'w_branch_diff': w((DEPTH, DIFF_V_COLS, D_MODEL), DIFF_V_COLS, BETA),
        'w_out': w((DEPTH, D_MODEL, D_MODEL), D_MODEL, BETA),
        'ln2_g': gain(D_MODEL),
        'ln2_b': bias(D_MODEL),
        'ffn2_w_gate': w((DEPTH, D_MODEL, D_FF), D_MODEL, BETA),
        'ffn2_w_up': w((DEPTH, D_MODEL, D_FF), D_MODEL, BETA),
        'ffn2_w_down': w((DEPTH, D_FF, D_MODEL), D_FF, BETA),
        'ln3_g': gain(D_MODEL),
        'ln3_b': bias(D_MODEL),
    }


def reference(x, positions, ln1_g, ln1_b, ffn1_w_gate, ffn1_w_up, ffn1_w_down, w_in,
              mla_q_norm_g, mla_w_uq, mla_kv_norm_g, mla_w_uk, mla_w_uv,
              diff_lambda_q1, diff_lambda_k1, diff_lambda_q2, diff_lambda_k2, diff_subln_g,
              w_branch_mla, w_branch_diff, w_out, ln2_g, ln2_b,
              ffn2_w_gate, ffn2_w_up, ffn2_w_down, ln3_g, ln3_b):
    h = x
    for l in range(DEPTH):
        lambda_init = 0.8 - 0.6 * math.exp(-0.3 * l)
        h = _layer_norm(ALPHA * h + 0.5 * _swiglu(h, ffn1_w_gate[l], ffn1_w_up[l], ffn1_w_down[l]),
                        ln1_g[l], ln1_b[l])
        mix = _mixer(h, positions, w_in[l], mla_q_norm_g[l], mla_w_uq[l], mla_kv_norm_g[l],
                     mla_w_uk[l], mla_w_uv[l], diff_lambda_q1[l], diff_lambda_k1[l],
                     diff_lambda_q2[l], diff_lambda_k2[l], diff_subln_g[l],
                     w_branch_mla[l], w_branch_diff[l], w_out[l], lambda_init)
        h = _layer_norm(ALPHA * h + mix, ln2_g[l], ln2_b[l])
        h = _layer_norm(ALPHA * h + 0.5 * _swiglu(h, ffn2_w_gate[l], ffn2_w_up[l], ffn2_w_down[l]),
                        ln3_g[l], ln3_b[l])
    return h
```

```python
import functools
import math

import jax
import jax.numpy as jnp
from jax import lax
from jax.experimental import pallas as pl
from jax.experimental.pallas import tpu as pltpu

D_MODEL = 2048
DEPTH = 1
MLA_HEADS = 8
MLA_Q_RANK = 512
MLA_KV_RANK = 512
MLA_NOPE_DIM = 128
MLA_ROPE_DIM = 64
MLA_V_DIM = 128
MLA_QK_DIM = MLA_NOPE_DIM + MLA_ROPE_DIM
MLA_QK_PAD = 256
ROPE_THETA = 10000.0
DIFF_HEADS = 4
DIFF_QK_DIM = 128
DIFF_V_DIM = 2 * DIFF_QK_DIM
DIFF_QK_COLS = DIFF_HEADS * 2 * DIFF_QK_DIM
DIFF_V_COLS = DIFF_HEADS * DIFF_V_DIM
D_FF = 5632
LN_EPS = 1e-5
RMS_EPS = 1e-6
ALPHA = (2 * DEPTH) ** 0.25

LANE = 128
VMEM_LIMIT_BYTES = 56 * 1024 * 1024

F32 = jnp.float32
BF16 = jnp.bfloat16


def _dot(a, b):
    return jnp.dot(a, b, preferred_element_type=F32)


def _dot_nt(a, b):
    return lax.dot_general(a, b, (((1,), (1,)), ((), ())), preferred_element_type=F32)


def _layer_norm_rows(y, g, b):
    mu = jnp.mean(y, axis=-1, keepdims=True)
    d = y - mu
    var = jnp.mean(d * d, axis=-1, keepdims=True)
    return d * lax.rsqrt(var + LN_EPS) * g + b


def _rms_norm_rows(y, g):
    ms = jnp.mean(y * y, axis=-1, keepdims=True)
    return y * lax.rsqrt(ms + RMS_EPS) * g


def _resident(shape):
    return pl.BlockSpec(shape, lambda *_: (0,) * len(shape), pipeline_mode=pl.Buffered(1))


def _ffn_ln_kernel(x_ref, wg_ref, wu_ref, wd_ref, g_ref, b_ref, o_ref, xb_ref):
    j = pl.program_id(1)

    @pl.when(j == 0)
    def _():
        x = x_ref[...]
        xb_ref[...] = x.astype(BF16)
        o_ref[...] = ALPHA * x

    xb = xb_ref[...]
    gate = _dot(xb, wg_ref[...])
    up = _dot(xb, wu_ref[...])
    act = (0.5 * (gate * jax.nn.sigmoid(gate))) * up
    o_ref[...] += _dot(act.astype(BF16), wd_ref[...])

    @pl.when(j == pl.num_programs(1) - 1)
    def _():
        o_ref[...] = _layer_norm_rows(o_ref[...], g_ref[...], b_ref[...])


def _ffn_ln(x, wg, wu, wd, g, b, *, tm=512, tf=512):
    t, d = x.shape
    f = wg.shape[1]
    return pl.pallas_call(
        _ffn_ln_kernel,
        grid=(t // tm, f // tf),
        in_specs=[
            pl.BlockSpec((tm, d), lambda i, j: (i, 0)),
            pl.BlockSpec((d, tf), lambda i, j: (0, j)),
            pl.BlockSpec((d, tf), lambda i, j: (0, j)),
            pl.BlockSpec((tf, d), lambda i, j: (j, 0)),
            pl.BlockSpec((1, d), lambda i, j: (0, 0)),
            pl.BlockSpec((1, d), lambda i, j: (0, 0)),
        ],
        out_specs=pl.BlockSpec((tm, d), lambda i, j: (i, 0)),
        out_shape=jax.ShapeDtypeStruct((t, d), F32),
        scratch_shapes=[pltpu.VMEM((tm, d), BF16)],
        compiler_params=pltpu.CompilerParams(
            dimension_semantics=("parallel", "arbitrary"),
            vmem_limit_bytes=VMEM_LIMIT_BYTES),
        name="ffn_ln",
    )(x, wg, wu, wd, g, b)


def _mixer_proj_kernel(h_ref, pos_ref, invf_ref, wlat_ref, wdiff_ref, qg_ref, kvg_ref,
                       wq_ref, wqp_ref, wuk_ref, wuv_ref,
                       q_ref, k_ref, v_ref, dq_ref, dk_ref, dv_ref):
    hb = h_ref[...].astype(BF16)

    dq_ref[...] = _dot(hb, wdiff_ref[:, 0:DIFF_QK_COLS]).astype(BF16)
    dk_ref[...] = _dot(hb, wdiff_ref[:, DIFF_QK_COLS:2 * DIFF_QK_COLS]).astype(BF16)
    dv_ref[...] = _dot(hb, wdiff_ref[:, 2 * DIFF_QK_COLS:]).astype(BF16)

    lat = _dot(hb, wlat_ref[...])
    c_q = lat[:, 0:MLA_Q_RANK]
    c_kv = lat[:, MLA_Q_RANK:MLA_Q_RANK + MLA_KV_RANK]
    base = MLA_Q_RANK + MLA_KV_RANK
    k_r = lat[:, base:base + LANE]
    k_r_rot = lat[:, base + LANE:base + 2 * LANE]

    ang = pos_ref[...].astype(F32) * invf_ref[...]
    cos = jnp.cos(ang)
    sin = jnp.sin(ang)

    cqn = _rms_norm_rows(c_q, qg_ref[...]).astype(BF16)
    ckvn = _rms_norm_rows(c_kv, kvg_ref[...]).astype(BF16)

    q = _dot(cqn, wq_ref[...])
    qp = _dot(cqn, wqp_ref[...])
    k_nope = _dot(ckvn, wuk_ref[...])
    v_ref[...] = _dot(ckvn, wuv_ref[...]).astype(BF16)

    k_rope = (k_r * cos + k_r_rot * sin).astype(BF16)
    for h in range(MLA_HEADS):
        c0 = h * MLA_QK_PAD
        q_ref[:, c0:c0 + LANE] = q[:, c0:c0 + LANE].astype(BF16)
        q_rope = q[:, c0 + LANE:c0 + 2 * LANE] * cos + qp[:, h * LANE:(h + 1) * LANE] * sin
        q_ref[:, c0 + LANE:c0 + 2 * LANE] = q_rope.astype(BF16)
        k_ref[:, c0:c0 + LANE] = k_nope[:, h * LANE:(h + 1) * LANE].astype(BF16)
        k_ref[:, c0 + LANE:c0 + 2 * LANE] = k_rope


def _mixer_proj(h, pos, invf, wlat, wdiff, qg, kvg, wq, wqp, wuk, wuv, *, tm=256):
    t, d = h.shape
    qk_cols = MLA_HEADS * MLA_QK_PAD
    v_cols = MLA_HEADS * MLA_V_DIM
    row = lambda n: pl.BlockSpec((tm, n), lambda i: (i, 0))
    outs = (qk_cols, qk_cols, v_cols, DIFF_QK_COLS, DIFF_QK_COLS, DIFF_V_COLS)
    return pl.pallas_call(
        _mixer_proj_kernel,
        grid=(t // tm,),
        in_specs=[row(d), row(1), _resident(invf.shape), _resident(wlat.shape),
                  _resident(wdiff.shape), _resident(qg.shape), _resident(kvg.shape),
                  _resident(wq.shape), _resident(wqp.shape), _resident(wuk.shape),
                  _resident(wuv.shape)],
        out_specs=[row(n) for n in outs],
        out_shape=[jax.ShapeDtypeStruct((t, n), BF16) for n in outs],
        compiler_params=pltpu.CompilerParams(
            dimension_semantics=("parallel",), vmem_limit_bytes=VMEM_LIMIT_BYTES),
        name="mixer_proj",
    )(h, pos, invf, wlat, wdiff, qg, kvg, wq, wqp, wuk, wuv)


def _causal_keep(tq, tk):
    rows = lax.broadcasted_iota(jnp.int32, (tq, tk), 0)
    cols = lax.broadcasted_iota(jnp.int32, (tq, tk), 1)
    return cols <= rows


def _online_softmax_step(s, v, m_ref, l_ref, acc_ref):
    m_prev = m_ref[...]
    m_new = jnp.maximum(m_prev, jnp.max(s, axis=-1, keepdims=True))
    corr = jnp.exp(m_prev - m_new)
    p = jnp.exp(s - m_new)
    l_ref[...] = corr * l_ref[...] + jnp.sum(p, axis=-1, keepdims=True)
    acc_ref[...] = corr * acc_ref[...] + _dot(p.astype(BF16), v)
    m_ref[...] = m_new


def _mla_attn_kernel(q_ref, k_ref, v_ref, o_ref, m_ref, l_ref, acc_ref, *, tq):
    qi = pl.program_id(2)
    scale = MLA_QK_DIM ** -0.5
    q = q_ref[...]
    m_ref[...] = jnp.full(m_ref.shape, -jnp.inf, F32)
    l_ref[...] = jnp.zeros(l_ref.shape, F32)
    acc_ref[...] = jnp.zeros(acc_ref.shape, F32)

    def body(c, carry):
        rows = pl.ds(pl.multiple_of(c * tq, tq), tq)
        s = _dot_nt(q, k_ref[rows, :]) * scale
        _online_softmax_step(s, v_ref[rows, :], m_ref, l_ref, acc_ref)
        return carry

    lax.fori_loop(0, qi, body, 0)

    rows = pl.ds(pl.multiple_of(qi * tq, tq), tq)
    s = _dot_nt(q, k_ref[rows, :]) * scale
    s = jnp.where(_causal_keep(tq, tq), s, -jnp.inf)
    _online_softmax_step(s, v_ref[rows, :], m_ref, l_ref, acc_ref)
    o_ref[...] = (acc_ref[...] / l_ref[...]).astype(o_ref.dtype)


def _mla_attn(q, k, v, *, tq=512):
    b, s, _ = q.shape
    return pl.pallas_call(
        functools.partial(_mla_attn_kernel, tq=tq),
        grid=(b, MLA_HEADS, s // tq),
        in_specs=[
            pl.BlockSpec((None, tq, MLA_QK_PAD), lambda bi, h, i: (bi, i, h)),
            pl.BlockSpec((None, s, MLA_QK_PAD), lambda bi, h, i: (bi, 0, h)),
            pl.BlockSpec((None, s, MLA_V_DIM), lambda bi, h, i: (bi, 0, h)),
        ],
        out_specs=pl.BlockSpec((None, tq, MLA_V_DIM), lambda bi, h, i: (bi, i, h)),
        out_shape=jax.ShapeDtypeStruct((b, s, MLA_HEADS * MLA_V_DIM), BF16),
        scratch_shapes=[pltpu.VMEM((tq, 1), F32), pltpu.VMEM((tq, 1), F32),
                        pltpu.VMEM((tq, MLA_V_DIM), F32)],
        compiler_params=pltpu.CompilerParams(
            dimension_semantics=("parallel", "parallel", "arbitrary"),
            vmem_limit_bytes=VMEM_LIMIT_BYTES),
        name="mla_attn",
    )(q, k, v)


def _diff_attn_kernel(slopes_ref, q_ref, k_ref, v_ref, posq_ref, posk_ref,
                      lq1_ref, lk1_ref, lq2_ref, lk2_ref, sg_ref, o_ref,
                      m0_ref, l0_ref, acc0_ref, m1_ref, l1_ref, acc1_ref,
                      *, tq, lambda_init):
    h = pl.program_id(1)
    qi = pl.program_id(2)
    scale = DIFF_QK_DIM ** -0.5
    slope = slopes_ref[h]
    stats = ((m0_ref, l0_ref, acc0_ref), (m1_ref, l1_ref, acc1_ref))
    for m_ref, l_ref, acc_ref in stats:
        m_ref[...] = jnp.full(m_ref.shape, -jnp.inf, F32)
        l_ref[...] = jnp.zeros(l_ref.shape, F32)
        acc_ref[...] = jnp.zeros(acc_ref.shape, F32)
    posq = posq_ref[...].astype(F32)

    def chunk(c, masked):
        rows = pl.ds(pl.multiple_of(c * tq, tq), tq)
        posk = posk_ref[c].astype(F32)
        bias = slope * jnp.abs(posq - posk)
        v = v_ref[rows, :]
        for mi, (m_ref, l_ref, acc_ref) in enumerate(stats):
            cols = slice(mi * DIFF_QK_DIM, (mi + 1) * DIFF_QK_DIM)
            s = _dot_nt(q_ref[:, cols], k_ref[rows, cols]) * scale - bias
            if masked:
                s = jnp.where(_causal_keep(tq, tq), s, -jnp.inf)
            _online_softmax_step(s, v, m_ref, l_ref, acc_ref)

    def body(c, carry):
        chunk(c, False)
        return carry

    lax.fori_loop(0, qi, body, 0)
    chunk(qi, True)

    lam = (jnp.exp(jnp.sum(lq1_ref[...] * lk1_ref[...], axis=-1, keepdims=True))
           - jnp.exp(jnp.sum(lq2_ref[...] * lk2_ref[...], axis=-1, keepdims=True))
           + lambda_init)
    o = acc0_ref[...] / l0_ref[...] - lam * (acc1_ref[...] / l1_ref[...])
    o = _rms_norm_rows(o, sg_ref[...]) * (1.0 - lambda_init)
    o_ref[...] = o.astype(o_ref.dtype)


def _diff_attn(slopes, dq, dk, dv, positions, lq1, lk1, lq2, lk2, sg, *, lambda_init, tq=512):
    b, s, _ = dq.shape
    hd = 2 * DIFF_QK_DIM
    posq = positions.reshape(b, s, 1)
    posk = positions.reshape(b, s // tq, 1, tq)
    vec = lambda n: pl.BlockSpec((1, n), lambda bi, h, i: (0, 0))
    return pl.pallas_call(
        functools.partial(_diff_attn_kernel, tq=tq, lambda_init=lambda_init),
        grid=(b, DIFF_HEADS, s // tq),
        in_specs=[
            pl.BlockSpec(memory_space=pltpu.SMEM),
            pl.BlockSpec((None, tq, hd), lambda bi, h, i: (bi, i, h)),
            pl.BlockSpec((None, s, hd), lambda bi, h, i: (bi, 0, h)),
            pl.BlockSpec((None, s, DIFF_V_DIM), lambda bi, h, i: (bi, 0, h)),
            pl.BlockSpec((None, tq, 1), lambda bi, h, i: (bi, i, 0)),
            pl.BlockSpec((None, s // tq, 1, tq), lambda bi, h, i: (bi, 0, 0, 0)),
            vec(DIFF_QK_DIM), vec(DIFF_QK_DIM), vec(DIFF_QK_DIM), vec(DIFF_QK_DIM),
            vec(DIFF_V_DIM),
        ],
        out_specs=pl.BlockSpec((None, tq, DIFF_V_DIM), lambda bi, h, i: (bi, i, h)),
        out_shape=jax.ShapeDtypeStruct((b, s, DIFF_V_COLS), BF16),
        scratch_shapes=[pltpu.VMEM((tq, 1), F32), pltpu.VMEM((tq, 1), F32),
                        pltpu.VMEM((tq, DIFF_V_DIM), F32),
                        pltpu.VMEM((tq, 1), F32), pltpu.VMEM((tq, 1), F32),
                        pltpu.VMEM((tq, DIFF_V_DIM), F32)],
        compiler_params=pltpu.CompilerParams(
            dimension_semantics=("parallel", "parallel", "arbitrary"),
            vmem_limit_bytes=VMEM_LIMIT_BYTES),
        name="diff_attn",
    )(slopes, dq, dk, dv, posq, posk, lq1, lk1, lq2, lk2, sg)


def _merge_ln_kernel(h_ref, om_ref, od_ref, wgm_ref, wgd_ref, wbm_ref, wbd_ref, wo_ref,
                     g_ref, b_ref, o_ref, y_ref, *, tn):
    h = h_ref[...]
    hb = h.astype(BF16)
    om = om_ref[...]
    od = od_ref[...]
    for c in range(D_MODEL // tn):
        cols = slice(c * tn, (c + 1) * tn)
        gm = jax.nn.sigmoid(_dot(hb, wgm_ref[:, cols]))
        gd = jax.nn.sigmoid(_dot(hb, wgd_ref[:, cols]))
        y = gm * _dot(om, wbm_ref[:, cols]) + gd * _dot(od, wbd_ref[:, cols])
        y_ref[:, cols] = y.astype(BF16)
    mix = _dot(y_ref[...], wo_ref[...])
    o_ref[...] = _layer_norm_rows(ALPHA * h + mix, g_ref[...], b_ref[...])


def _merge_ln(h, om, od, wgm, wgd, wbm, wbd, wo, g, b, *, tm=256, tn=512):
    t, d = h.shape
    row = lambda n: pl.BlockSpec((tm, n), lambda i: (i, 0))
    return pl.pallas_call(
        functools.partial(_merge_ln_kernel, tn=tn),
        grid=(t // tm,),
        in_specs=[row(d), row(om.shape[1]), row(od.shape[1]),
                  _resident(wgm.shape), _resident(wgd.shape), _resident(wbm.shape),
                  _resident(wbd.shape), _resident(wo.shape),
                  _resident(g.shape), _resident(b.shape)],
        out_specs=row(d),
        out_shape=jax.ShapeDtypeStruct((t, d), F32),
        scratch_shapes=[pltpu.VMEM((tm, d), BF16)],
        compiler_params=pltpu.CompilerParams(
            dimension_semantics=("parallel",), vmem_limit_bytes=VMEM_LIMIT_BYTES),
        name="merge_ln",
    )(h, om, od, wgm, wgd, wbm, wbd, wo, g, b)


def _rotate_half_cols(w):
    half = w.shape[-1] // 2
    return jnp.concatenate([-w[..., half:], w[..., :half]], axis=-1)


def _prep_mixer_weights(w_in, w_uq):
    d = w_in.shape[0]
    base = MLA_Q_RANK + MLA_KV_RANK
    w_kr = w_in[:, base:base + MLA_ROPE_DIM]
    zpad = jnp.zeros((d, LANE - MLA_ROPE_DIM), w_in.dtype)
    wlat = jnp.concatenate([w_in[:, :base], w_kr, zpad, _rotate_half_cols(w_kr), zpad], axis=1)
    d0 = base + MLA_ROPE_DIM
    wdiff = w_in[:, d0:d0 + 2 * DIFF_QK_COLS + DIFF_V_COLS]
    g0 = d0 + 2 * DIFF_QK_COLS + DIFF_V_COLS
    wgm = w_in[:, g0:g0 + D_MODEL]
    wgd = w_in[:, g0 + D_MODEL:g0 + 2 * D_MODEL]

    r = w_uq.shape[0]
    wq3 = w_uq.reshape(r, MLA_HEADS, MLA_QK_DIM)
    nope, rope = wq3[..., :MLA_NOPE_DIM], wq3[..., MLA_NOPE_DIM:]
    z3 = jnp.zeros((r, MLA_HEADS, LANE - MLA_ROPE_DIM), w_uq.dtype)
    wq = jnp.concatenate([nope, rope, z3], axis=-1).reshape(r, MLA_HEADS * MLA_QK_PAD)
    wqp = jnp.concatenate([_rotate_half_cols(rope), z3], axis=-1).reshape(r, MLA_HEADS * LANE)
    return wlat, wdiff, wgm, wgd, wq, wqp


def kernel(x, positions, ln1_g, ln1_b, ffn1_w_gate, ffn1_w_up, ffn1_w_down, w_in, mla_q_norm_g, mla_w_uq, mla_kv_norm_g, mla_w_uk, mla_w_uv, diff_lambda_q1, diff_lambda_k1, diff_lambda_q2, diff_lambda_k2, diff_subln_g, w_branch_mla, w_branch_diff, w_out, ln2_g, ln2_b, ffn2_w_gate, ffn2_w_up, ffn2_w_down, ln3_g, ln3_b):
    b, s, d = x.shape
    t = b * s
    bf = lambda w: w.astype(BF16)

    half = MLA_ROPE_DIM // 2
    inv_freq = ROPE_THETA ** (-jnp.arange(half, dtype=F32) / half)
    invf = jnp.concatenate([inv_freq, inv_freq, jnp.zeros((LANE - MLA_ROPE_DIM,), F32)])[None, :]
    slopes = 2.0 ** (-8.0 * jnp.arange(1, DIFF_HEADS + 1, dtype=F32) / DIFF_HEADS)

    h = x.reshape(t, d)
    for l in range(DEPTH):
        lambda_init = 0.8 - 0.6 * math.exp(-0.3 * l)
        h = _ffn_ln(h, bf(ffn1_w_gate[l]), bf(ffn1_w_up[l]), bf(ffn1_w_down[l]),
                    ln1_g[l][None, :], ln1_b[l][None, :])

        wlat, wdiff, wgm, wgd, wq, wqp = _prep_mixer_weights(w_in[l], mla_w_uq[l])
        q, k, v, dq, dk, dv = _mixer_proj(
            h, positions.reshape(t, 1), invf, bf(wlat), bf(wdiff),
            mla_q_norm_g[l][None, :], mla_kv_norm_g[l][None, :],
            bf(wq), bf(wqp), bf(mla_w_uk[l]), bf(mla_w_uv[l]))

        o_mla = _mla_attn(q.reshape(b, s, -1), k.reshape(b, s, -1), v.reshape(b, s, -1))
        o_diff = _diff_attn(
            slopes, dq.reshape(b, s, -1), dk.reshape(b, s, -1), dv.reshape(b, s, -1),
            positions,
            diff_lambda_q1[l][None, :], diff_lambda_k1[l][None, :],
            diff_lambda_q2[l][None, :], diff_lambda_k2[l][None, :],
            diff_subln_g[l][None, :], lambda_init=lambda_init)

        h = _merge_ln(h, o_mla.reshape(t, -1), o_diff.reshape(t, -1), bf(wgm), bf(wgd),
                      bf(w_branch_mla[l]), bf(w_branch_diff[l]), bf(w_out[l]),
                      ln2_g[l][None, :], ln2_b[l][None, :])

        h = _ffn_ln(h, bf(ffn2_w_gate[l]), bf(ffn2_w_up[l]), bf(ffn2_w_down[l]),
                    ln3_g[l][None, :], ln3_b[l][None, :])
    return h.reshape(b, s, d)
```

```python
import functools
import math

import jax
import jax.numpy as jnp
from jax import lax
from jax.experimental import pallas as pl
from jax.experimental.pallas import tpu as pltpu

D_MODEL = 2048
DEPTH = 1
MLA_HEADS = 8
MLA_Q_RANK = 512
MLA_KV_RANK = 512
MLA_NOPE_DIM = 128
MLA_ROPE_DIM = 64
MLA_V_DIM = 128
MLA_QK_DIM = MLA_NOPE_DIM + MLA_ROPE_DIM
MLA_QK_PAD = 256
ROPE_THETA = 10000.0
DIFF_HEADS = 4
DIFF_QK_DIM = 128
DIFF_V_DIM = 2 * DIFF_QK_DIM
DIFF_QK_COLS = DIFF_HEADS * 2 * DIFF_QK_DIM
DIFF_V_COLS = DIFF_HEADS * DIFF_V_DIM
D_FF = 5632
LN_EPS = 1e-5
RMS_EPS = 1e-6
ALPHA = (2 * DEPTH) ** 0.25
LOG2E = math.log2(math.e)

LANE = 128
MXU_DIM = 256
VMEM_LIMIT_BYTES = 56 * 1024 * 1024
PROJ_TM = 256
ATTN_TQ = 1024
ATTN_TK = 512

F32 = jnp.float32
BF16 = jnp.bfloat16


def _dot(a, b):
    return jnp.dot(a, b, preferred_element_type=F32)


def _dot_nt(a, b):
    return lax.dot_general(a, b, (((1,), (1,)), ((), ())), preferred_element_type=F32)


def _layer_norm_rows(y, g, b):
    mu = jnp.mean(y, axis=-1, keepdims=True)
    d = y - mu
    var = jnp.mean(d * d, axis=-1, keepdims=True)
    return d * lax.rsqrt(var + LN_EPS) * g + b


def _rms_norm_rows(y, g):
    ms = jnp.mean(y * y, axis=-1, keepdims=True)
    return y * lax.rsqrt(ms + RMS_EPS) * g


def _resident(shape):
    return pl.BlockSpec(shape, lambda *_: (0,) * len(shape), pipeline_mode=pl.Buffered(1))


def _ffn_ln_kernel(x_ref, wg_ref, wu_ref, wd_ref, g_ref, b_ref, o_ref, xb_ref):
    j = pl.program_id(1)

    @pl.when(j == 0)
    def _():
        x = x_ref[...]
        xb_ref[...] = x.astype(BF16)
        o_ref[...] = ALPHA * x

    xb = xb_ref[...]
    gate = _dot(xb, wg_ref[...])
    up = _dot(xb, wu_ref[...])
    act = (0.5 * (gate * jax.nn.sigmoid(gate))) * up
    o_ref[...] += _dot(act.astype(BF16), wd_ref[...])

    @pl.when(j == pl.num_programs(1) - 1)
    def _():
        o_ref[...] = _layer_norm_rows(o_ref[...], g_ref[...], b_ref[...])


def _ffn_ln(x, wg, wu, wd, g, b, *, tm=512, tf=512):
    t, d = x.shape
    f = wg.shape[1]
    return pl.pallas_call(
        _ffn_ln_kernel,
        grid=(t // tm, f // tf),
        in_specs=[
            pl.BlockSpec((tm, d), lambda i, j: (i, 0)),
            pl.BlockSpec((d, tf), lambda i, j: (0, j)),
            pl.BlockSpec((d, tf), lambda i, j: (0, j)),
            pl.BlockSpec((tf, d), lambda i, j: (j, 0)),
            pl.BlockSpec((1, d), lambda i, j: (0, 0)),
            pl.BlockSpec((1, d), lambda i, j: (0, 0)),
        ],
        out_specs=pl.BlockSpec((tm, d), lambda i, j: (i, 0)),
        out_shape=jax.ShapeDtypeStruct((t, d), F32),
        scratch_shapes=[pltpu.VMEM((tm, d), BF16)],
        compiler_params=pltpu.CompilerParams(
            dimension_semantics=("parallel", "arbitrary"),
            vmem_limit_bytes=VMEM_LIMIT_BYTES),
        name="ffn_ln",
    )(x, wg, wu, wd, g, b)


def _mixer_proj_kernel(h_ref, pos_ref, invf_ref, wlat_ref, wdiff_ref, wdvt_ref, qg_ref, kvg_ref,
                       wq_ref, wqp_ref, wuk_ref, wuvt_ref,
                       q_ref, k_ref, vt_ref, dq_ref, dk_ref, dvt_ref):
    hb = h_ref[...].astype(BF16)

    dq_scale = DIFF_QK_DIM ** -0.5 * LOG2E
    dq_ref[...] = (_dot(hb, wdiff_ref[:, 0:DIFF_QK_COLS]) * dq_scale).astype(BF16)
    dk_ref[...] = _dot(hb, wdiff_ref[:, DIFF_QK_COLS:2 * DIFF_QK_COLS]).astype(BF16)
    dvt_ref[...] = _dot_nt(wdvt_ref[...], hb).astype(BF16)

    lat = _dot(hb, wlat_ref[...])
    c_q = lat[:, 0:MLA_Q_RANK]
    c_kv = lat[:, MLA_Q_RANK:MLA_Q_RANK + MLA_KV_RANK]
    base = MLA_Q_RANK + MLA_KV_RANK
    k_r = lat[:, base:base + LANE]
    k_r_rot = lat[:, base + LANE:base + 2 * LANE]

    ang = pos_ref[...].astype(F32) * invf_ref[...]
    cos = jnp.cos(ang)
    sin = jnp.sin(ang)

    cqn = _rms_norm_rows(c_q, qg_ref[...]).astype(BF16)
    ckvn = _rms_norm_rows(c_kv, kvg_ref[...]).astype(BF16)

    q = _dot(cqn, wq_ref[...])
    qp = _dot(cqn, wqp_ref[...])
    k_nope = _dot(ckvn, wuk_ref[...])
    vt_ref[...] = _dot_nt(wuvt_ref[...], ckvn).astype(BF16)

    q_scale = MLA_QK_DIM ** -0.5 * LOG2E
    k_rope = (k_r * cos + k_r_rot * sin).astype(BF16)
    for h in range(MLA_HEADS):
        c0 = h * MLA_QK_PAD
        q_ref[:, c0:c0 + LANE] = (q[:, c0:c0 + LANE] * q_scale).astype(BF16)
        q_rope = q[:, c0 + LANE:c0 + 2 * LANE] * cos + qp[:, h * LANE:(h + 1) * LANE] * sin
        q_ref[:, c0 + LANE:c0 + 2 * LANE] = (q_rope * q_scale).astype(BF16)
        k_ref[:, c0:c0 + LANE] = k_nope[:, h * LANE:(h + 1) * LANE].astype(BF16)
        k_ref[:, c0 + LANE:c0 + 2 * LANE] = k_rope


def _mixer_proj(h, pos, invf, wlat, wdiff, wdvt, qg, kvg, wq, wqp, wuk, wuvt):
    t, d = h.shape
    tm = PROJ_TM
    qk_cols = MLA_HEADS * MLA_QK_PAD
    v_cols = MLA_HEADS * MLA_V_DIM
    row = lambda n: pl.BlockSpec((tm, n), lambda i: (i, 0))
    piece = lambda n: pl.BlockSpec((None, n, tm), lambda i: (i, 0, 0))
    rows_out = lambda n: jax.ShapeDtypeStruct((t, n), BF16)
    piece_out = lambda n: jax.ShapeDtypeStruct((t // tm, n, tm), BF16)
    return pl.pallas_call(
        _mixer_proj_kernel,
        grid=(t // tm,),
        in_specs=[row(d), row(1), _resident(invf.shape), _resident(wlat.shape),
                  _resident(wdiff.shape), _resident(wdvt.shape),
                  _resident(qg.shape), _resident(kvg.shape),
                  _resident(wq.shape), _resident(wqp.shape), _resident(wuk.shape),
                  _resident(wuvt.shape)],
        out_specs=[row(qk_cols), row(qk_cols), piece(v_cols),
                   row(DIFF_QK_COLS), row(DIFF_QK_COLS), piece(DIFF_V_COLS)],
        out_shape=[rows_out(qk_cols), rows_out(qk_cols), piece_out(v_cols),
                   rows_out(DIFF_QK_COLS), rows_out(DIFF_QK_COLS), piece_out(DIFF_V_COLS)],
        compiler_params=pltpu.CompilerParams(
            dimension_semantics=("parallel",), vmem_limit_bytes=VMEM_LIMIT_BYTES),
        name="mixer_proj",
    )(h, pos, invf, wlat, wdiff, wdvt, qg, kvg, wq, wqp, wuk, wuvt)


def _diag_units(tq, tk):
    units = []
    for j in range(tq // tk):
        for g in range(tq // MXU_DIM):
            if j * tk >= (g + 1) * MXU_DIM:
                continue
            visible = (j + 1) * tk <= g * MXU_DIM + 1
            units.append((j, g, None if visible else j * tk))
    return units


def _causal_mask_t(s, g, key_off):
    rows = lax.broadcasted_iota(jnp.int32, s.shape, 0)
    cols = lax.broadcasted_iota(jnp.int32, s.shape, 1)
    return jnp.where(rows + key_off <= cols + g * MXU_DIM, s, -jnp.inf)


def _run_pipelined(units, score_fn, consume_fn):
    s_next = score_fn(units[0])
    for i, u in enumerate(units):
        s_cur = s_next
        if i + 1 < len(units):
            s_next = score_fn(units[i + 1])
        consume_fn(u, s_cur)


def _softmax_step_t(s, vt_pieces, m_ref, l_ref, acc_ref, cols):
    m_prev = m_ref[:, cols]
    m_new = jnp.maximum(m_prev, jnp.max(s, axis=0, keepdims=True))
    corr = jnp.exp2(m_prev - m_new)
    p = jnp.exp2(s - m_new)
    l_ref[:, cols] = corr * l_ref[:, cols] + jnp.sum(p, axis=0, keepdims=True)
    pb = p.astype(BF16)
    pv = None
    for u, vt in enumerate(vt_pieces):
        part = _dot(vt, pb[u * MXU_DIM:(u + 1) * MXU_DIM, :])
        pv = part if pv is None else pv + part
    acc_ref[:, cols] = corr * acc_ref[:, cols] + pv
    m_ref[:, cols] = m_new


def _init_stats(m_ref, l_ref, acc_ref):
    m_ref[...] = jnp.full(m_ref.shape, -jnp.inf, F32)
    l_ref[...] = jnp.zeros(l_ref.shape, F32)
    acc_ref[...] = jnp.zeros(acc_ref.shape, F32)


def _mla_attn_kernel(q_ref, k_ref, vt_ref, o_ref, m_ref, l_ref, acc_ref, *, tq, tk):
    qi = pl.program_id(2)
    n_piece = tk // MXU_DIM
    n_group = tq // MXU_DIM
    ratio = tq // tk
    _init_stats(m_ref, l_ref, acc_ref)

    def score(unit):
        c, g, key_off = unit
        kc = k_ref[pl.ds(pl.multiple_of(c * tk, tk), tk), :]
        s = _dot_nt(kc, q_ref[g * MXU_DIM:(g + 1) * MXU_DIM, :])
        return s if key_off is None else _causal_mask_t(s, g, key_off)

    def consume(unit, s):
        c, g, _ = unit
        vts = [vt_ref[c * n_piece + u] for u in range(n_piece)]
        _softmax_step_t(s, vts, m_ref, l_ref, acc_ref, slice(g * MXU_DIM, (g + 1) * MXU_DIM))

    def body(c, carry):
        _run_pipelined([(c, g, None) for g in range(n_group)], score, consume)
        return carry

    lax.fori_loop(0, qi * ratio, body, 0)
    _run_pipelined([(qi * ratio + j, g, off) for j, g, off in _diag_units(tq, tk)], score, consume)
    o = acc_ref[...] * (1.0 / l_ref[...])
    o_ref[...] = o.T.astype(o_ref.dtype)


def _mla_attn(q, k, vt):
    b, s, _ = q.shape
    tq, tk = ATTN_TQ, ATTN_TK
    return pl.pallas_call(
        functools.partial(_mla_attn_kernel, tq=tq, tk=tk),
        grid=(b, MLA_HEADS, s // tq),
        in_specs=[
            pl.BlockSpec((None, tq, MLA_QK_PAD), lambda bi, h, i: (bi, i, h)),
            pl.BlockSpec((None, s, MLA_QK_PAD), lambda bi, h, i: (bi, 0, h)),
            pl.BlockSpec((None, s // PROJ_TM, MLA_V_DIM, PROJ_TM), lambda bi, h, i: (bi, 0, h, 0)),
        ],
        out_specs=pl.BlockSpec((None, tq, MLA_V_DIM), lambda bi, h, i: (bi, i, h)),
        out_shape=jax.ShapeDtypeStruct((b, s, MLA_HEADS * MLA_V_DIM), BF16),
        scratch_shapes=[pltpu.VMEM((1, tq), F32), pltpu.VMEM((1, tq), F32),
                        pltpu.VMEM((MLA_V_DIM, tq), F32)],
        compiler_params=pltpu.CompilerParams(
            dimension_semantics=("parallel", "parallel", "arbitrary"),
            vmem_limit_bytes=VMEM_LIMIT_BYTES),
        name="mla_attn",
    )(q, k, vt)


def _diff_attn_kernel(slopes_ref, q_ref, k_ref, vt_ref, posq_ref, posk_ref,
                      lq1_ref, lk1_ref, lq2_ref, lk2_ref, sg_ref, o_ref,
                      m0_ref, l0_ref, acc0_ref, m1_ref, l1_ref, acc1_ref,
                      *, tq, tk, lambda_init):
    h = pl.program_id(1)
    qi = pl.program_id(2)
    n_piece = tk // MXU_DIM
    n_group = tq // MXU_DIM
    ratio = tq // tk
    slope = slopes_ref[h] * LOG2E
    stats = ((m0_ref, l0_ref, acc0_ref), (m1_ref, l1_ref, acc1_ref))
    for m_ref, l_ref, acc_ref in stats:
        _init_stats(m_ref, l_ref, acc_ref)

    def run(chunk_units):
        bias_cache = {}

        def score(unit):
            ci, mi = unit
            c, g, key_off = chunk_units[ci]
            rows = pl.ds(pl.multiple_of(c * tk, tk), tk)
            if ci not in bias_cache:
                posk = posk_ref[rows, :].astype(F32)
                posq = posq_ref[:, g * MXU_DIM:(g + 1) * MXU_DIM].astype(F32)
                bias_cache[ci] = slope * jnp.abs(posk - posq)
            dcols = slice(mi * DIFF_QK_DIM, (mi + 1) * DIFF_QK_DIM)
            s = _dot_nt(k_ref[rows, dcols], q_ref[g * MXU_DIM:(g + 1) * MXU_DIM, dcols])
            s = s - bias_cache[ci]
            return s if key_off is None else _causal_mask_t(s, g, key_off)

        def consume(unit, s):
            ci, mi = unit
            c, g, _ = chunk_units[ci]
            m_ref, l_ref, acc_ref = stats[mi]
            vts = [vt_ref[c * n_piece + u] for u in range(n_piece)]
            _softmax_step_t(s, vts, m_ref, l_ref, acc_ref, slice(g * MXU_DIM, (g + 1) * MXU_DIM))

        _run_pipelined([(ci, mi) for ci in range(len(chunk_units)) for mi in range(2)],
                       score, consume)

    def body(c, carry):
        run([(c, g, None) for g in range(n_group)])
        return carry

    lax.fori_loop(0, qi * ratio, body, 0)
    run([(qi * ratio + j, g, off) for j, g, off in _diag_units(tq, tk)])

    lam = (jnp.exp(jnp.sum(lq1_ref[...] * lk1_ref[...], axis=-1, keepdims=True))
           - jnp.exp(jnp.sum(lq2_ref[...] * lk2_ref[...], axis=-1, keepdims=True))
           + lambda_init)
    o_t = acc0_ref[...] * (1.0 / l0_ref[...]) - lam * (acc1_ref[...] * (1.0 / l1_ref[...]))
    o = _rms_norm_rows(o_t.T, sg_ref[...]) * (1.0 - lambda_init)
    o_ref[...] = o.astype(o_ref.dtype)


def _diff_attn(slopes, dq, dk, dvt, positions, lq1, lk1, lq2, lk2, sg, *, lambda_init):
    b, s, _ = dq.shape
    tq, tk = ATTN_TQ, ATTN_TK
    hd = 2 * DIFF_QK_DIM
    posq = positions.reshape(b, 1, s)
    posk = positions.reshape(b, s, 1)
    vec = lambda n: pl.BlockSpec((1, n), lambda bi, h, i: (0, 0))
    return pl.pallas_call(
        functools.partial(_diff_attn_kernel, tq=tq, tk=tk, lambda_init=lambda_init),
        grid=(b, DIFF_HEADS, s // tq),
        in_specs=[
            pl.BlockSpec(memory_space=pltpu.SMEM),
            pl.BlockSpec((None, tq, hd), lambda bi, h, i: (bi, i, h)),
            pl.BlockSpec((None, s, hd), lambda bi, h, i: (bi, 0, h)),
            pl.BlockSpec((None, s // PROJ_TM, DIFF_V_DIM, PROJ_TM), lambda bi, h, i: (bi, 0, h, 0)),
            pl.BlockSpec((None, 1, tq), lambda bi, h, i: (bi, 0, i)),
            pl.BlockSpec((None, s, 1), lambda bi, h, i: (bi, 0, 0)),
            vec(DIFF_QK_DIM), vec(DIFF_QK_DIM), vec(DIFF_QK_DIM), vec(DIFF_QK_DIM),
            vec(DIFF_V_DIM),
        ],
        out_specs=pl.BlockSpec((None, tq, DIFF_V_DIM), lambda bi, h, i: (bi, i, h)),
        out_shape=jax.ShapeDtypeStruct((b, s, DIFF_V_COLS), BF16),
        scratch_shapes=[pltpu.VMEM((1, tq), F32), pltpu.VMEM((1, tq), F32),
                        pltpu.VMEM((DIFF_V_DIM, tq), F32),
                        pltpu.VMEM((1, tq), F32), pltpu.VMEM((1, tq), F32),
                        pltpu.VMEM((DIFF_V_DIM, tq), F32)],
        compiler_params=pltpu.CompilerParams(
            dimension_semantics=("parallel", "parallel", "arbitrary"),
            vmem_limit_bytes=VMEM_LIMIT_BYTES),
        name="diff_attn",
    )(slopes, dq, dk, dvt, posq, posk, lq1, lk1, lq2, lk2, sg)


def _merge_ln_kernel(h_ref, om_ref, od_ref, wgm_ref, wgd_ref, wbm_ref, wbd_ref, wo_ref,
                     g_ref, b_ref, o_ref, y_ref, *, tn):
    h = h_ref[...]
    hb = h.astype(BF16)
    om = om_ref[...]
    od = od_ref[...]
    for c in range(D_MODEL // tn):
        cols = slice(c * tn, (c + 1) * tn)
        gm = jax.nn.sigmoid(_dot(hb, wgm_ref[:, cols]))
        gd = jax.nn.sigmoid(_dot(hb, wgd_ref[:, cols]))
        y = gm * _dot(om, wbm_ref[:, cols]) + gd * _dot(od, wbd_ref[:, cols])
        y_ref[:, cols] = y.astype(BF16)
    mix = _dot(y_ref[...], wo_ref[...])
    o_ref[...] = _layer_norm_rows(ALPHA * h + mix, g_ref[...], b_ref[...])


def _merge_ln(h, om, od, wgm, wgd, wbm, wbd, wo, g, b, *, tm=256, tn=512):
    t, d = h.shape
    row = lambda n: pl.BlockSpec((tm, n), lambda i: (i, 0))
    return pl.pallas_call(
        functools.partial(_merge_ln_kernel, tn=tn),
        grid=(t // tm,),
        in_specs=[row(d), row(om.shape[1]), row(od.shape[1]),
                  _resident(wgm.shape), _resident(wgd.shape), _resident(wbm.shape),
                  _resident(wbd.shape), _resident(wo.shape),
                  _resident(g.shape), _resident(b.shape)],
        out_specs=row(d),
        out_shape=jax.ShapeDtypeStruct((t, d), F32),
        scratch_shapes=[pltpu.VMEM((tm, d), BF16)],
        compiler_params=pltpu.CompilerParams(
            dimension_semantics=("parallel",), vmem_limit_bytes=VMEM_LIMIT_BYTES),
        name="merge_ln",
    )(h, om, od, wgm, wgd, wbm, wbd, wo, g, b)


def _rotate_half_cols(w):
    half = w.shape[-1] // 2
    return jnp.concatenate([-w[..., half:], w[..., :half]], axis=-1)


def _prep_mixer_weights(w_in, w_uq):
    d = w_in.shape[0]
    base = MLA_Q_RANK + MLA_KV_RANK
    w_kr = w_in[:, base:base + MLA_ROPE_DIM]
    zpad = jnp.zeros((d, LANE - MLA_ROPE_DIM), w_in.dtype)
    wlat = jnp.concatenate([w_in[:, :base], w_kr, zpad, _rotate_half_cols(w_kr), zpad], axis=1)
    d0 = base + MLA_ROPE_DIM
    wdiff = w_in[:, d0:d0 + 2 * DIFF_QK_COLS]
    v0 = d0 + 2 * DIFF_QK_COLS
    wdvt = w_in[:, v0:v0 + DIFF_V_COLS].T
    g0 = v0 + DIFF_V_COLS
    wgm = w_in[:, g0:g0 + D_MODEL]
    wgd = w_in[:, g0 + D_MODEL:g0 + 2 * D_MODEL]

    r = w_uq.shape[0]
    wq3 = w_uq.reshape(r, MLA_HEADS, MLA_QK_DIM)
    nope, rope = wq3[..., :MLA_NOPE_DIM], wq3[..., MLA_NOPE_DIM:]
    z3 = jnp.zeros((r, MLA_HEADS, LANE - MLA_ROPE_DIM), w_uq.dtype)
    wq = jnp.concatenate([nope, rope, z3], axis=-1).reshape(r, MLA_HEADS * MLA_QK_PAD)
    wqp = jnp.concatenate([_rotate_half_cols(rope), z3], axis=-1).reshape(r, MLA_HEADS * LANE)
    return wlat, wdiff, wdvt, wgm, wgd, wq, wqp


def kernel(x, positions, ln1_g, ln1_b, ffn1_w_gate, ffn1_w_up, ffn1_w_down, w_in, mla_q_norm_g, mla_w_uq, mla_kv_norm_g, mla_w_uk, mla_w_uv, diff_lambda_q1, diff_lambda_k1, diff_lambda_q2, diff_lambda_k2, diff_subln_g, w_branch_mla, w_branch_diff, w_out, ln2_g, ln2_b, ffn2_w_gate, ffn2_w_up, ffn2_w_down, ln3_g, ln3_b):
    b, s, d = x.shape
    t = b * s
    bf = lambda w: w.astype(BF16)

    half = MLA_ROPE_DIM // 2
    inv_freq = ROPE_THETA ** (-jnp.arange(half, dtype=F32) / half)
    invf = jnp.concatenate([inv_freq, inv_freq, jnp.zeros((LANE - MLA_ROPE_DIM,), F32)])[None, :]
    slopes = 2.0 ** (-8.0 * jnp.arange(1, DIFF_HEADS + 1, dtype=F32) / DIFF_HEADS)

    h = x.reshape(t, d)
    for l in range(DEPTH):
        lambda_init = 0.8 - 0.6 * math.exp(-0.3 * l)
        h = _ffn_ln(h, bf(ffn1_w_gate[l]), bf(ffn1_w_up[l]), bf(ffn1_w_down[l]),
                    ln1_g[l][None, :], ln1_b[l][None, :])

        wlat, wdiff, wdvt, wgm, wgd, wq, wqp = _prep_mixer_weights(w_in[l], mla_w_uq[l])
        q, k, vt, dq, dk, dvt = _mixer_proj(
            h, positions.reshape(t, 1), invf, bf(wlat), bf(wdiff), bf(wdvt),
            mla_q_norm_g[l][None, :], mla_kv_norm_g[l][None, :],
            bf(wq), bf(wqp), bf(mla_w_uk[l]), bf(mla_w_uv[l].T))

        pieces = s // PROJ_TM
        o_mla = _mla_attn(q.reshape(b, s, -1), k.reshape(b, s, -1),
                          vt.reshape(b, pieces, -1, PROJ_TM))
        o_diff = _diff_attn(
            slopes, dq.reshape(b, s, -1), dk.reshape(b, s, -1),
            dvt.reshape(b, pieces, -1, PROJ_TM), positions,
            diff_lambda_q1[l][None, :], diff_lambda_k1[l][None, :],
            diff_lambda_q2[l][None, :], diff_lambda_k2[l][None, :],
            diff_subln_g[l][None, :], lambda_init=lambda_init)

        h = _merge_ln(h, o_mla.reshape(t, -1), o_diff.reshape(t, -1), bf(wgm), bf(wgd),
                      bf(w_branch_mla[l]), bf(w_branch_diff[l]), bf(w_out[l]),
                      ln2_g[l][None, :], ln2_b[l][None, :])

        h = _ffn_ln(h, bf(ffn2_w_gate[l]), bf(ffn2_w_up[l]), bf(ffn2_w_down[l]),
                    ln3_g[l][None, :], ln3_b[l][None, :])
    return h.reshape(b, s, d)
```

```python
import functools
import math

import jax
import jax.numpy as jnp
from jax import lax
from jax.experimental import pallas as pl
from jax.experimental.pallas import tpu as pltpu

D_MODEL = 2048
DEPTH = 1
MLA_HEADS = 8
MLA_Q_RANK = 512
MLA_KV_RANK = 512
MLA_NOPE_DIM = 128
MLA_ROPE_DIM = 64
MLA_V_DIM = 128
MLA_QK_DIM = MLA_NOPE_DIM + MLA_ROPE_DIM
MLA_QK_PAD = 256
ROPE_THETA = 10000.0
DIFF_HEADS = 4
DIFF_QK_DIM = 128
DIFF_V_DIM = 2 * DIFF_QK_DIM
DIFF_QK_COLS = DIFF_HEADS * 2 * DIFF_QK_DIM
DIFF_V_COLS = DIFF_HEADS * DIFF_V_DIM
D_FF = 5632
LN_EPS = 1e-5
RMS_EPS = 1e-6
ALPHA = (2 * DEPTH) ** 0.25
LOG2E = math.log2(math.e)

LANE = 128
MXU_DIM = 256
ONES_ROWS = 16
VMEM_LIMIT_BYTES = 56 * 1024 * 1024
PROJ_TM = MXU_DIM
ATTN_TQ = 1024
ATTN_TK = 512
ATTN_GW = 2 * MXU_DIM
LOOKAHEAD = 2
SCORE_RING = 4

F32 = jnp.float32
BF16 = jnp.bfloat16


def _dot(a, b):
    return jnp.dot(a, b, preferred_element_type=F32)


def _dot_nt(a, b):
    return lax.dot_general(a, b, (((1,), (1,)), ((), ())), preferred_element_type=F32)


def _layer_norm_rows(y, g, b):
    mu = jnp.mean(y, axis=-1, keepdims=True)
    d = y - mu
    var = jnp.mean(d * d, axis=-1, keepdims=True)
    return d * lax.rsqrt(var + LN_EPS) * g + b


def _rms_norm_rows(y, g):
    ms = jnp.mean(y * y, axis=-1, keepdims=True)
    return y * lax.rsqrt(ms + RMS_EPS) * g


def _resident(shape):
    return pl.BlockSpec(shape, lambda *_: (0,) * len(shape), pipeline_mode=pl.Buffered(1))


def _ffn_ln_kernel(x_ref, wg_ref, wu_ref, wd_ref, g_ref, b_ref, o_ref, xb_ref):
    j = pl.program_id(1)

    @pl.when(j == 0)
    def _():
        x = x_ref[...]
        xb_ref[...] = x.astype(BF16)
        o_ref[...] = ALPHA * x

    xb = xb_ref[...]
    gate = _dot(xb, wg_ref[...])
    up = _dot(xb, wu_ref[...])
    act = (0.5 * (gate * jax.nn.sigmoid(gate))) * up
    o_ref[...] += _dot(act.astype(BF16), wd_ref[...])

    @pl.when(j == pl.num_programs(1) - 1)
    def _():
        o_ref[...] = _layer_norm_rows(o_ref[...], g_ref[...], b_ref[...])


def _ffn_ln(x, wg, wu, wd, g, b, *, tm=512, tf=512):
    t, d = x.shape
    f = wg.shape[1]
    return pl.pallas_call(
        _ffn_ln_kernel,
        grid=(t // tm, f // tf),
        in_specs=[
            pl.BlockSpec((tm, d), lambda i, j: (i, 0)),
            pl.BlockSpec((d, tf), lambda i, j: (0, j)),
            pl.BlockSpec((d, tf), lambda i, j: (0, j)),
            pl.BlockSpec((tf, d), lambda i, j: (j, 0)),
            pl.BlockSpec((1, d), lambda i, j: (0, 0)),
            pl.BlockSpec((1, d), lambda i, j: (0, 0)),
        ],
        out_specs=pl.BlockSpec((tm, d), lambda i, j: (i, 0)),
        out_shape=jax.ShapeDtypeStruct((t, d), F32),
        scratch_shapes=[pltpu.VMEM((tm, d), BF16)],
        compiler_params=pltpu.CompilerParams(
            dimension_semantics=("parallel", "arbitrary"),
            vmem_limit_bytes=VMEM_LIMIT_BYTES),
        name="ffn_ln",
    )(x, wg, wu, wd, g, b)


def _mixer_proj_kernel(h_ref, pos_ref, invf_ref, wlat_ref, wdiff_ref, wdvt_ref, qg_ref, kvg_ref,
                       wq_ref, wqp_ref, wuk_ref, wuvt_ref,
                       q_ref, k_ref, vt_ref, dq_ref, dk_ref, dvt_ref):
    hb = h_ref[...].astype(BF16)

    dq_scale = DIFF_QK_DIM ** -0.5 * LOG2E
    dq_ref[...] = (_dot(hb, wdiff_ref[:, 0:DIFF_QK_COLS]) * dq_scale).astype(BF16)
    dk_ref[...] = _dot(hb, wdiff_ref[:, DIFF_QK_COLS:2 * DIFF_QK_COLS]).astype(BF16)
    ones_rows = (lax.broadcasted_iota(jnp.int32, (ONES_ROWS, hb.shape[0]), 0) == 0).astype(BF16)
    dvt = _dot_nt(wdvt_ref[...], hb).astype(BF16)
    for h in range(DIFF_HEADS):
        dvt_ref[h, 0:DIFF_V_DIM, :] = dvt[h * DIFF_V_DIM:(h + 1) * DIFF_V_DIM, :]
        dvt_ref[h, DIFF_V_DIM:, :] = ones_rows

    lat = _dot(hb, wlat_ref[...])
    c_q = lat[:, 0:MLA_Q_RANK]
    c_kv = lat[:, MLA_Q_RANK:MLA_Q_RANK + MLA_KV_RANK]
    base = MLA_Q_RANK + MLA_KV_RANK
    k_r = lat[:, base:base + LANE]
    k_r_rot = lat[:, base + LANE:base + 2 * LANE]

    ang = pos_ref[...] * invf_ref[...]
    cos = jnp.cos(ang)
    sin = jnp.sin(ang)

    cqn = _rms_norm_rows(c_q, qg_ref[...]).astype(BF16)
    ckvn = _rms_norm_rows(c_kv, kvg_ref[...]).astype(BF16)

    q = _dot(cqn, wq_ref[...])
    qp = _dot(cqn, wqp_ref[...])
    k_nope = _dot(ckvn, wuk_ref[...])
    vt = _dot_nt(wuvt_ref[...], ckvn).astype(BF16)
    for h in range(MLA_HEADS):
        vt_ref[h, 0:MLA_V_DIM, :] = vt[h * MLA_V_DIM:(h + 1) * MLA_V_DIM, :]
        vt_ref[h, MLA_V_DIM:, :] = ones_rows

    q_scale = MLA_QK_DIM ** -0.5 * LOG2E
    k_rope = (k_r * cos + k_r_rot * sin).astype(BF16)
    for h in range(MLA_HEADS):
        c0 = h * MLA_QK_PAD
        q_ref[:, c0:c0 + LANE] = (q[:, c0:c0 + LANE] * q_scale).astype(BF16)
        q_rope = q[:, c0 + LANE:c0 + 2 * LANE] * cos + qp[:, h * LANE:(h + 1) * LANE] * sin
        q_ref[:, c0 + LANE:c0 + 2 * LANE] = (q_rope * q_scale).astype(BF16)
        k_ref[:, c0:c0 + LANE] = k_nope[:, h * LANE:(h + 1) * LANE].astype(BF16)
        k_ref[:, c0 + LANE:c0 + 2 * LANE] = k_rope


def _mixer_proj(h, pos, invf, wlat, wdiff, wdvt, qg, kvg, wq, wqp, wuk, wuvt):
    t, d = h.shape
    tm = PROJ_TM
    qk_cols = MLA_HEADS * MLA_QK_PAD
    row = lambda n: pl.BlockSpec((tm, n), lambda i: (i, 0))
    piece = lambda nh, dv: pl.BlockSpec((None, nh, dv + ONES_ROWS, tm), lambda i: (i, 0, 0, 0))
    rows_out = lambda n: jax.ShapeDtypeStruct((t, n), BF16)
    piece_out = lambda nh, dv: jax.ShapeDtypeStruct((t // tm, nh, dv + ONES_ROWS, tm), BF16)
    return pl.pallas_call(
        _mixer_proj_kernel,
        grid=(t // tm,),
        in_specs=[row(d), row(1), _resident(invf.shape), _resident(wlat.shape),
                  _resident(wdiff.shape), _resident(wdvt.shape),
                  _resident(qg.shape), _resident(kvg.shape),
                  _resident(wq.shape), _resident(wqp.shape), _resident(wuk.shape),
                  _resident(wuvt.shape)],
        out_specs=[row(qk_cols), row(qk_cols), piece(MLA_HEADS, MLA_V_DIM),
                   row(DIFF_QK_COLS), row(DIFF_QK_COLS), piece(DIFF_HEADS, DIFF_V_DIM)],
        out_shape=[rows_out(qk_cols), rows_out(qk_cols), piece_out(MLA_HEADS, MLA_V_DIM),
                   rows_out(DIFF_QK_COLS), rows_out(DIFF_QK_COLS),
                   piece_out(DIFF_HEADS, DIFF_V_DIM)],
        compiler_params=pltpu.CompilerParams(
            dimension_semantics=("parallel",), vmem_limit_bytes=VMEM_LIMIT_BYTES),
        name="mixer_proj",
    )(h, pos, invf, wlat, wdiff, wdvt, qg, kvg, wq, wqp, wuk, wuvt)


def _diag_units(tq, tk):
    units = []
    for j in range(tq // tk):
        for g in range(tq // ATTN_GW):
            if j * tk >= (g + 1) * ATTN_GW:
                continue
            visible = (j + 1) * tk <= g * ATTN_GW + 1
            units.append((j, g, None if visible else j * tk))
    return units


def _causal_mask_t(s, g, key_off):
    rows = lax.broadcasted_iota(jnp.int32, s.shape, 0)
    cols = lax.broadcasted_iota(jnp.int32, s.shape, 1)
    return jnp.where(rows + key_off <= cols + g * ATTN_GW, s, -jnp.inf)


def _run_pipelined(units, s_refs, score_fn, consume_fn, next_units=()):
    ring = len(s_refs)
    todo = list(enumerate(list(units) + list(next_units)))[LOOKAHEAD:]
    for k, u in enumerate(units):
        if todo:
            kk, nu = todo.pop(0)
            s_refs[kk % ring][...] = score_fn(nu)
        consume_fn(u, s_refs[k % ring][...])


def _prime_pipeline(units, s_refs, score_fn):
    for k, u in enumerate(units[:LOOKAHEAD]):
        s_refs[k][...] = score_fn(u)


def _softmax_step_t(s, vt_pieces, m_ref, acc_ref, cols):
    m_prev = m_ref[:, cols]
    m_new = jnp.maximum(m_prev, jnp.max(s, axis=0, keepdims=True))
    corr = jnp.exp2(m_prev - m_new)
    p = jnp.exp2((s - m_new).astype(BF16))
    pv = None
    for u, vt in enumerate(vt_pieces):
        part = _dot(vt, p[u * PROJ_TM:(u + 1) * PROJ_TM, :])
        pv = part if pv is None else pv + part
    acc_ref[:, cols] = corr * acc_ref[:, cols] + pv
    m_ref[:, cols] = m_new


def _init_stats(m_ref, acc_ref):
    m_ref[...] = jnp.full(m_ref.shape, -jnp.inf, F32)
    acc_ref[...] = jnp.zeros(acc_ref.shape, F32)


def _normalized(acc_ref, dv):
    return acc_ref[0:dv, :] * (1.0 / acc_ref[dv:dv + 1, :])


def _mla_attn_kernel(q_ref, k_ref, vt_ref, o_ref, m_ref, acc_ref, *s_refs, tq, tk):
    qi = pl.program_id(2)
    n_piece = tk // PROJ_TM
    n_group = tq // ATTN_GW
    n_full = qi * (tq // tk)
    _init_stats(m_ref, acc_ref)

    def score(unit):
        c, g, _ = unit
        kc = k_ref[pl.ds(pl.multiple_of(c * tk, tk), tk), :]
        return _dot_nt(kc, q_ref[g * ATTN_GW:(g + 1) * ATTN_GW, :])

    def consume(unit, s):
        c, g, key_off = unit
        if key_off is not None:
            s = _causal_mask_t(s, g, key_off)
        vts = [vt_ref[c * n_piece + u] for u in range(n_piece)]
        _softmax_step_t(s, vts, m_ref, acc_ref, slice(g * ATTN_GW, (g + 1) * ATTN_GW))

    step = max(1, len(s_refs) // n_group)
    assert (step * n_group) % len(s_refs) == 0 and (tq // tk) % step == 0

    def chunk_units(c0):
        return [(c0 + dc, g, None) for dc in range(step) for g in range(n_group)]

    def body(i, carry):
        c0 = i * step
        _run_pipelined(chunk_units(c0), s_refs, score, consume, next_units=chunk_units(c0 + step))
        return carry

    _prime_pipeline(chunk_units(0), s_refs, score)
    lax.fori_loop(0, n_full // step, body, 0)
    diag = [(n_full + j, g, off) for j, g, off in _diag_units(tq, tk)]
    _run_pipelined(diag, s_refs, score, consume)
    o_ref[...] = _normalized(acc_ref, MLA_V_DIM).T.astype(o_ref.dtype)


def _mla_attn(q, k, vt):
    b, s, _ = q.shape
    tq, tk = ATTN_TQ, ATTN_TK
    rows = MLA_V_DIM + ONES_ROWS
    return pl.pallas_call(
        functools.partial(_mla_attn_kernel, tq=tq, tk=tk),
        grid=(b, MLA_HEADS, s // tq),
        in_specs=[
            pl.BlockSpec((None, tq, MLA_QK_PAD), lambda bi, h, i: (bi, i, h)),
            pl.BlockSpec((None, s, MLA_QK_PAD), lambda bi, h, i: (bi, 0, h)),
            pl.BlockSpec((None, s // PROJ_TM, None, rows, PROJ_TM),
                         lambda bi, h, i: (bi, 0, h, 0, 0)),
        ],
        out_specs=pl.BlockSpec((None, tq, MLA_V_DIM), lambda bi, h, i: (bi, i, h)),
        out_shape=jax.ShapeDtypeStruct((b, s, MLA_HEADS * MLA_V_DIM), BF16),
        scratch_shapes=[pltpu.VMEM((1, tq), F32), pltpu.VMEM((rows, tq), F32)]
        + [pltpu.VMEM((tk, ATTN_GW), F32)] * SCORE_RING,
        compiler_params=pltpu.CompilerParams(
            dimension_semantics=("parallel", "parallel", "arbitrary"),
            vmem_limit_bytes=VMEM_LIMIT_BYTES),
        name="mla_attn",
    )(q, k, vt)


def _diff_attn_kernel(slopes_ref, q_ref, k_ref, vt_ref, posq_ref, posk_ref,
                      lq1_ref, lk1_ref, lq2_ref, lk2_ref, sg_ref, o_ref,
                      m0_ref, acc0_ref, m1_ref, acc1_ref, *s_refs, tq, tk, lambda_init):
    h = pl.program_id(1)
    qi = pl.program_id(2)
    n_piece = tk // PROJ_TM
    n_group = tq // ATTN_GW
    n_full = qi * (tq // tk)
    slope = slopes_ref[h] * LOG2E
    stats = ((m0_ref, acc0_ref), (m1_ref, acc1_ref))
    for m_ref, acc_ref in stats:
        _init_stats(m_ref, acc_ref)

    def make_fns():
        bias_cache = {}

        def score(unit):
            ckey, c, g, mi, _ = unit
            rows = pl.ds(pl.multiple_of(c * tk, tk), tk)
            if (ckey, g) not in bias_cache:
                posk = posk_ref[rows, :]
                posq = posq_ref[:, g * ATTN_GW:(g + 1) * ATTN_GW]
                bias_cache[(ckey, g)] = slope * jnp.abs(posk - posq)
            dcols = slice(mi * DIFF_QK_DIM, (mi + 1) * DIFF_QK_DIM)
            s = _dot_nt(k_ref[rows, dcols], q_ref[g * ATTN_GW:(g + 1) * ATTN_GW, dcols])
            return s - bias_cache[(ckey, g)]

        def consume(unit, s):
            _, c, g, mi, key_off = unit
            if key_off is not None:
                s = _causal_mask_t(s, g, key_off)
            m_ref, acc_ref = stats[mi]
            vts = [vt_ref[c * n_piece + u] for u in range(n_piece)]
            _softmax_step_t(s, vts, m_ref, acc_ref, slice(g * ATTN_GW, (g + 1) * ATTN_GW))

        return score, consume

    def chunk_units(ckey, c):
        return [(ckey, c, g, mi, None) for g in range(n_group) for mi in range(2)]

    assert (2 * n_group) % len(s_refs) == 0

    def body(c, carry):
        score, consume = make_fns()
        _run_pipelined(chunk_units("cur", c), s_refs, score, consume,
                       next_units=chunk_units("next", c + 1))
        return carry

    score, consume = make_fns()
    _prime_pipeline(chunk_units("first", 0), s_refs, score)
    lax.fori_loop(0, n_full, body, 0)
    diag = [(j, n_full + j, g, mi, off) for j, g, off in _diag_units(tq, tk) for mi in range(2)]
    _run_pipelined(diag, s_refs, score, consume)

    lam = (jnp.exp(jnp.sum(lq1_ref[...] * lk1_ref[...], axis=-1, keepdims=True))
           - jnp.exp(jnp.sum(lq2_ref[...] * lk2_ref[...], axis=-1, keepdims=True))
           + lambda_init)
    o_t = _normalized(acc0_ref, DIFF_V_DIM) - lam * _normalized(acc1_ref, DIFF_V_DIM)
    o = _rms_norm_rows(o_t.T, sg_ref[...]) * (1.0 - lambda_init)
    o_ref[...] = o.astype(o_ref.dtype)


def _diff_attn(slopes, dq, dk, dvt, positions, lq1, lk1, lq2, lk2, sg, *, lambda_init):
    b, s, _ = dq.shape
    tq, tk = ATTN_TQ, ATTN_TK
    hd = 2 * DIFF_QK_DIM
    rows = DIFF_V_DIM + ONES_ROWS
    posf = positions.astype(F32)
    posq = posf.reshape(b, 1, s)
    posk = posf.reshape(b, s, 1)
    vec = lambda n: pl.BlockSpec((1, n), lambda bi, h, i: (0, 0))
    return pl.pallas_call(
        functools.partial(_diff_attn_kernel, tq=tq, tk=tk, lambda_init=lambda_init),
        grid=(b, DIFF_HEADS, s // tq),
        in_specs=[
            pl.BlockSpec(memory_space=pltpu.SMEM),
            pl.BlockSpec((None, tq, hd), lambda bi, h, i: (bi, i, h)),
            pl.BlockSpec((None, s, hd), lambda bi, h, i: (bi, 0, h)),
            pl.BlockSpec((None, s // PROJ_TM, None, rows, PROJ_TM),
                         lambda bi, h, i: (bi, 0, h, 0, 0)),
            pl.BlockSpec((None, 1, tq), lambda bi, h, i: (bi, 0, i)),
            pl.BlockSpec((None, s, 1), lambda bi, h, i: (bi, 0, 0)),
            vec(DIFF_QK_DIM), vec(DIFF_QK_DIM), vec(DIFF_QK_DIM), vec(DIFF_QK_DIM),
            vec(DIFF_V_DIM),
        ],
        out_specs=pl.BlockSpec((None, tq, DIFF_V_DIM), lambda bi, h, i: (bi, i, h)),
        out_shape=jax.ShapeDtypeStruct((b, s, DIFF_V_COLS), BF16),
        scratch_shapes=[pltpu.VMEM((1, tq), F32), pltpu.VMEM((rows, tq), F32),
                        pltpu.VMEM((1, tq), F32), pltpu.VMEM((rows, tq), F32)]
        + [pltpu.VMEM((tk, ATTN_GW), F32)] * SCORE_RING,
        compiler_params=pltpu.CompilerParams(
            dimension_semantics=("parallel", "parallel", "arbitrary"),
            vmem_limit_bytes=VMEM_LIMIT_BYTES),
        name="diff_attn",
    )(slopes, dq, dk, dvt, posq, posk, lq1, lk1, lq2, lk2, sg)


def _merge_ln_kernel(h_ref, om_ref, od_ref, wgm_ref, wgd_ref, wbm_ref, wbd_ref, wo_ref,
                     g_ref, b_ref, o_ref, y_ref, *, tn):
    h = h_ref[...]
    hb = h.astype(BF16)
    om = om_ref[...]
    od = od_ref[...]
    for c in range(D_MODEL // tn):
        cols = slice(c * tn, (c + 1) * tn)
        gm = jax.nn.sigmoid(_dot(hb, wgm_ref[:, cols]))
        gd = jax.nn.sigmoid(_dot(hb, wgd_ref[:, cols]))
        y = gm * _dot(om, wbm_ref[:, cols]) + gd * _dot(od, wbd_ref[:, cols])
        y_ref[:, cols] = y.astype(BF16)
    mix = _dot(y_ref[...], wo_ref[...])
    o_ref[...] = _layer_norm_rows(ALPHA * h + mix, g_ref[...], b_ref[...])


def _merge_ln(h, om, od, wgm, wgd, wbm, wbd, wo, g, b, *, tm=256, tn=512):
    t, d = h.shape
    row = lambda n: pl.BlockSpec((tm, n), lambda i: (i, 0))
    return pl.pallas_call(
        functools.partial(_merge_ln_kernel, tn=tn),
        grid=(t // tm,),
        in_specs=[row(d), row(om.shape[1]), row(od.shape[1]),
                  _resident(wgm.shape), _resident(wgd.shape), _resident(wbm.shape),
                  _resident(wbd.shape), _resident(wo.shape),
                  _resident(g.shape), _resident(b.shape)],
        out_specs=row(d),
        out_shape=jax.ShapeDtypeStruct((t, d), F32),
        scratch_shapes=[pltpu.VMEM((tm, d), BF16)],
        compiler_params=pltpu.CompilerParams(
            dimension_semantics=("parallel",), vmem_limit_bytes=VMEM_LIMIT_BYTES),
        name="merge_ln",
    )(h, om, od, wgm, wgd, wbm, wbd, wo, g, b)


def _rotate_half_cols(w):
    half = w.shape[-1] // 2
    return jnp.concatenate([-w[..., half:], w[..., :half]], axis=-1)


def _prep_mixer_weights(w_in, w_uq):
    d = w_in.shape[0]
    base = MLA_Q_RANK + MLA_KV_RANK
    w_kr = w_in[:, base:base + MLA_ROPE_DIM]
    zpad = jnp.zeros((d, LANE - MLA_ROPE_DIM), w_in.dtype)
    wlat = jnp.concatenate([w_in[:, :base], w_kr, zpad, _rotate_half_cols(w_kr), zpad], axis=1)
    d0 = base + MLA_ROPE_DIM
    wdiff = w_in[:, d0:d0 + 2 * DIFF_QK_COLS]
    v0 = d0 + 2 * DIFF_QK_COLS
    wdvt = w_in[:, v0:v0 + DIFF_V_COLS].T
    g0 = v0 + DIFF_V_COLS
    wgm = w_in[:, g0:g0 + D_MODEL]
    wgd = w_in[:, g0 + D_MODEL:g0 + 2 * D_MODEL]

    r = w_uq.shape[0]
    wq3 = w_uq.reshape(r, MLA_HEADS, MLA_QK_DIM)
    nope, rope = wq3[..., :MLA_NOPE_DIM], wq3[..., MLA_NOPE_DIM:]
    z3 = jnp.zeros((r, MLA_HEADS, LANE - MLA_ROPE_DIM), w_uq.dtype)
    wq = jnp.concatenate([nope, rope, z3], axis=-1).reshape(r, MLA_HEADS * MLA_QK_PAD)
    wqp = jnp.concatenate([_rotate_half_cols(rope), z3], axis=-1).reshape(r, MLA_HEADS * LANE)
    return wlat, wdiff, wdvt, wgm, wgd, wq, wqp


def kernel(x, positions, ln1_g, ln1_b, ffn1_w_gate, ffn1_w_up, ffn1_w_down, w_in, mla_q_norm_g, mla_w_uq, mla_kv_norm_g, mla_w_uk, mla_w_uv, diff_lambda_q1, diff_lambda_k1, diff_lambda_q2, diff_lambda_k2, diff_subln_g, w_branch_mla, w_branch_diff, w_out, ln2_g, ln2_b, ffn2_w_gate, ffn2_w_up, ffn2_w_down, ln3_g, ln3_b):
    b, s, d = x.shape
    t = b * s
    bf = lambda w: w.astype(BF16)

    half = MLA_ROPE_DIM // 2
    inv_freq = ROPE_THETA ** (-jnp.arange(half, dtype=F32) / half)
    invf = jnp.concatenate([inv_freq, inv_freq, jnp.zeros((LANE - MLA_ROPE_DIM,), F32)])[None, :]
    slopes = 2.0 ** (-8.0 * jnp.arange(1, DIFF_HEADS + 1, dtype=F32) / DIFF_HEADS)

    h = x.reshape(t, d)
    for l in range(DEPTH):
        lambda_init = 0.8 - 0.6 * math.exp(-0.3 * l)
        h = _ffn_ln(h, bf(ffn1_w_gate[l]), bf(ffn1_w_up[l]), bf(ffn1_w_down[l]),
                    ln1_g[l][None, :], ln1_b[l][None, :])

        wlat, wdiff, wdvt, wgm, wgd, wq, wqp = _prep_mixer_weights(bf(w_in[l]), bf(mla_w_uq[l]))
        q, k, vt, dq, dk, dvt = _mixer_proj(
            h, positions.reshape(t, 1).astype(F32), invf, wlat, wdiff, wdvt,
            mla_q_norm_g[l][None, :], mla_kv_norm_g[l][None, :],
            wq, wqp, bf(mla_w_uk[l]), bf(mla_w_uv[l]).T)

        pieces = s // PROJ_TM
        o_mla = _mla_attn(q.reshape(b, s, -1), k.reshape(b, s, -1),
                          vt.reshape((b, pieces) + vt.shape[1:]))
        o_diff = _diff_attn(
            slopes, dq.reshape(b, s, -1), dk.reshape(b, s, -1),
            dvt.reshape((b, pieces) + dvt.shape[1:]), positions,
            diff_lambda_q1[l][None, :], diff_lambda_k1[l][None, :],
            diff_lambda_q2[l][None, :], diff_lambda_k2[l][None, :],
            diff_subln_g[l][None, :], lambda_init=lambda_init)

        h = _merge_ln(h, o_mla.reshape(t, -1), o_diff.reshape(t, -1), wgm, wgd,
                      bf(w_branch_mla[l]), bf(w_branch_diff[l]), bf(w_out[l]),
                      ln2_g[l][None, :], ln2_b[l][None, :])

        h = _ffn_ln(h, bf(ffn2_w_gate[l]), bf(ffn2_w_up[l]), bf(ffn2_w_down[l]),
                    ln3_g[l][None, :], ln3_b[l][None, :])
    return h.reshape(b, s, d)
```

```python
import functools
import math

import jax
import jax.numpy as jnp
from jax import lax
from jax.experimental import pallas as pl
from jax.experimental.pallas import tpu as pltpu

D_MODEL = 2048
DEPTH = 1
MLA_HEADS = 8
MLA_Q_RANK = 512
MLA_KV_RANK = 512
MLA_NOPE_DIM = 128
MLA_ROPE_DIM = 64
MLA_V_DIM = 128
MLA_QK_DIM = MLA_NOPE_DIM + MLA_ROPE_DIM
MLA_QK_PAD = 256
ROPE_THETA = 10000.0
DIFF_HEADS = 4
DIFF_QK_DIM = 128
DIFF_V_DIM = 2 * DIFF_QK_DIM
DIFF_QK_COLS = DIFF_HEADS * 2 * DIFF_QK_DIM
DIFF_V_COLS = DIFF_HEADS * DIFF_V_DIM
D_FF = 5632
LN_EPS = 1e-5
RMS_EPS = 1e-6
ALPHA = (2 * DEPTH) ** 0.25
LOG2E = math.log2(math.e)

LANE = 128
MXU_DIM = 256
ONES_ROWS = 16
VMEM_LIMIT_BYTES = 56 * 1024 * 1024
VMEM_INTERNAL_SCRATCH_BYTES = 8 * 1024 * 1024
PROJ_TM = MXU_DIM
ATTN_TQ = 2048
ATTN_TK = 512
ATTN_GW = 2 * MXU_DIM
GROUP_PIECES = ATTN_GW // PROJ_TM
LOOKAHEAD = 2
SCORE_RING = 4
MLA_CHUNKS_PER_ITER = 1

F32 = jnp.float32
BF16 = jnp.bfloat16


def _dot(a, b):
    return jnp.dot(a, b, preferred_element_type=F32)


def _dot_nt(a, b):
    return lax.dot_general(a, b, (((1,), (1,)), ((), ())), preferred_element_type=F32)


def _layer_norm_rows(y, g, b):
    mu = jnp.mean(y, axis=-1, keepdims=True)
    d = y - mu
    var = jnp.mean(d * d, axis=-1, keepdims=True)
    return d * lax.rsqrt(var + LN_EPS) * g + b


def _rms_norm_rows(y, g):
    ms = jnp.mean(y * y, axis=-1, keepdims=True)
    return y * lax.rsqrt(ms + RMS_EPS) * g


def _resident(shape):
    return pl.BlockSpec(shape, lambda *_: (0,) * len(shape), pipeline_mode=pl.Buffered(1))


def _ffn_ln_kernel(x_ref, wg_ref, wu_ref, wd_ref, g_ref, b_ref, o_ref, xb_ref):
    j = pl.program_id(1)

    @pl.when(j == 0)
    def _():
        x = x_ref[...]
        xb_ref[...] = x.astype(BF16)
        o_ref[...] = ALPHA * x

    xb = xb_ref[...]
    gate = _dot(xb, wg_ref[...])
    up = _dot(xb, wu_ref[...])
    act = (0.5 * (gate * jax.nn.sigmoid(gate))) * up
    o_ref[...] += _dot(act.astype(BF16), wd_ref[...])

    @pl.when(j == pl.num_programs(1) - 1)
    def _():
        o_ref[...] = _layer_norm_rows(o_ref[...], g_ref[...], b_ref[...])


def _ffn_ln(x, wg, wu, wd, g, b, *, tm=1024, tf=512):
    t, d = x.shape
    f = wg.shape[1]
    vmem_bytes = (2 * tm * d * 4) * 2 + tm * d * 2 + 2 * (3 * d * tf * 2) + tm * tf * (4 + 4 + 2)
    return pl.pallas_call(
        _ffn_ln_kernel,
        grid=(t // tm, f // tf),
        in_specs=[
            pl.BlockSpec((tm, d), lambda i, j: (i, 0)),
            pl.BlockSpec((d, tf), lambda i, j: (0, j)),
            pl.BlockSpec((d, tf), lambda i, j: (0, j)),
            pl.BlockSpec((tf, d), lambda i, j: (j, 0)),
            pl.BlockSpec((1, d), lambda i, j: (0, 0)),
            pl.BlockSpec((1, d), lambda i, j: (0, 0)),
        ],
        out_specs=pl.BlockSpec((tm, d), lambda i, j: (i, 0)),
        out_shape=jax.ShapeDtypeStruct((t, d), F32),
        scratch_shapes=[pltpu.VMEM((tm, d), BF16)],
        compiler_params=pltpu.CompilerParams(
            dimension_semantics=("parallel", "arbitrary"),
            vmem_limit_bytes=vmem_bytes + VMEM_INTERNAL_SCRATCH_BYTES),
        name="ffn_ln",
    )(x, wg, wu, wd, g, b)


def _mixer_proj_kernel(h_ref, pos_ref, invf_ref, wlat_ref, wdqt_ref, wdk_ref, wdvt_ref,
                       qg_ref, kvg_ref, wqt_ref, wqpt_ref, wuk_ref, wuvt_ref,
                       qt_ref, k_ref, vt_ref, dqt_ref, dk_ref, dvt_ref):
    hb = h_ref[...].astype(BF16)
    tm = hb.shape[0]

    dq_scale = DIFF_QK_DIM ** -0.5 * LOG2E
    dqt_ref[...] = (_dot_nt(wdqt_ref[...], hb) * dq_scale).astype(BF16)
    dk_ref[...] = _dot(hb, wdk_ref[...]).astype(BF16)
    ones_rows = (lax.broadcasted_iota(jnp.int32, (ONES_ROWS, tm), 0) == 0).astype(BF16)
    dvt = _dot_nt(wdvt_ref[...], hb).astype(BF16)
    for h in range(DIFF_HEADS):
        dvt_ref[h, 0:DIFF_V_DIM, :] = dvt[h * DIFF_V_DIM:(h + 1) * DIFF_V_DIM, :]
        dvt_ref[h, DIFF_V_DIM:, :] = ones_rows

    lat = _dot(hb, wlat_ref[...])
    c_q = lat[:, 0:MLA_Q_RANK]
    c_kv = lat[:, MLA_Q_RANK:MLA_Q_RANK + MLA_KV_RANK]
    base = MLA_Q_RANK + MLA_KV_RANK
    k_r = lat[:, base:base + LANE]
    k_r_rot = lat[:, base + LANE:base + 2 * LANE]

    ang = invf_ref[...] * pos_ref[...]
    cos_f, sin_f = jnp.cos(ang), jnp.sin(ang)
    pad = jnp.zeros((LANE - MLA_ROPE_DIM, tm), F32)
    cos_t = jnp.concatenate([cos_f, cos_f, pad], axis=0)
    sin_t = jnp.concatenate([sin_f, sin_f, pad], axis=0)
    cos, sin = cos_t.T, sin_t.T

    cqn = _rms_norm_rows(c_q, qg_ref[...]).astype(BF16)
    ckvn = _rms_norm_rows(c_kv, kvg_ref[...]).astype(BF16)

    q_t = _dot_nt(wqt_ref[...], cqn)
    qp_t = _dot_nt(wqpt_ref[...], cqn)
    k_nope = _dot(ckvn, wuk_ref[...])
    vt = _dot_nt(wuvt_ref[...], ckvn).astype(BF16)
    for h in range(MLA_HEADS):
        vt_ref[h, 0:MLA_V_DIM, :] = vt[h * MLA_V_DIM:(h + 1) * MLA_V_DIM, :]
        vt_ref[h, MLA_V_DIM:, :] = ones_rows

    q_scale = MLA_QK_DIM ** -0.5 * LOG2E
    k_rope = (k_r * cos + k_r_rot * sin).astype(BF16)
    for h in range(MLA_HEADS):
        c0 = h * MLA_QK_PAD
        qt_ref[h, 0:LANE, :] = (q_t[c0:c0 + LANE, :] * q_scale).astype(BF16)
        q_rope = q_t[c0 + LANE:c0 + 2 * LANE, :] * cos_t + qp_t[h * LANE:(h + 1) * LANE, :] * sin_t
        qt_ref[h, LANE:2 * LANE, :] = (q_rope * q_scale).astype(BF16)
        k_ref[:, c0:c0 + LANE] = k_nope[:, h * LANE:(h + 1) * LANE].astype(BF16)
        k_ref[:, c0 + LANE:c0 + 2 * LANE] = k_rope


def _mixer_proj(h, pos, invf, wlat, wdqt, wdk, wdvt, qg, kvg, wqt, wqpt, wuk, wuvt):
    t, d = h.shape
    tm = PROJ_TM
    qk_cols = MLA_HEADS * MLA_QK_PAD
    row = lambda n: pl.BlockSpec((tm, n), lambda i: (i, 0))
    tile_t = lambda *dims: pl.BlockSpec((None,) + dims + (tm,),
                                        lambda i: (i,) + (0,) * (len(dims) + 1))
    rows_out = lambda n: jax.ShapeDtypeStruct((t, n), BF16)
    tile_t_out = lambda *dims: jax.ShapeDtypeStruct((t // tm,) + dims + (tm,), BF16)
    weights = (invf, wlat, wdqt, wdk, wdvt, qg, kvg, wqt, wqpt, wuk, wuvt)
    return pl.pallas_call(
        _mixer_proj_kernel,
        grid=(t // tm,),
        in_specs=[row(d), tile_t(1)] + [_resident(w.shape) for w in weights],
        out_specs=[tile_t(MLA_HEADS, MLA_QK_PAD), row(qk_cols),
                   tile_t(MLA_HEADS, MLA_V_DIM + ONES_ROWS),
                   tile_t(DIFF_QK_COLS), row(DIFF_QK_COLS),
                   tile_t(DIFF_HEADS, DIFF_V_DIM + ONES_ROWS)],
        out_shape=[tile_t_out(MLA_HEADS, MLA_QK_PAD), rows_out(qk_cols),
                   tile_t_out(MLA_HEADS, MLA_V_DIM + ONES_ROWS),
                   tile_t_out(DIFF_QK_COLS), rows_out(DIFF_QK_COLS),
                   tile_t_out(DIFF_HEADS, DIFF_V_DIM + ONES_ROWS)],
        compiler_params=pltpu.CompilerParams(
            dimension_semantics=("parallel",), vmem_limit_bytes=VMEM_LIMIT_BYTES),
        name="mixer_proj",
    )(h, pos, *weights)


def _diag_units(tq, tk):
    units = []
    for j in range(tq // tk):
        for g in range(tq // ATTN_GW):
            if j * tk >= (g + 1) * ATTN_GW:
                continue
            visible = (j + 1) * tk <= g * ATTN_GW + 1
            units.append((j, g, None if visible else j * tk))
    return units


def _causal_mask_t(s, g, key_off):
    rows = lax.broadcasted_iota(jnp.int32, s.shape, 0)
    cols = lax.broadcasted_iota(jnp.int32, s.shape, 1)
    return jnp.where(rows + key_off <= cols + g * ATTN_GW, s, -jnp.inf)


def _run_pipelined(units, s_refs, score_fn, consume_fn, next_units=()):
    ring = len(s_refs)
    todo = list(enumerate(list(units) + list(next_units)))[LOOKAHEAD:]
    for k, u in enumerate(units):
        if todo:
            kk, nu = todo.pop(0)
            score_fn(nu, s_refs[kk % ring])
        consume_fn(u, s_refs[k % ring][...])


def _prime_pipeline(units, s_refs, score_fn):
    for k, u in enumerate(units[:LOOKAHEAD]):
        score_fn(u, s_refs[k])


def _softmax_step_t(s, vt_pieces, m_ref, acc_ref, cols):
    m_prev = m_ref[:, cols]
    m_new = jnp.maximum(m_prev, jnp.max(s, axis=0, keepdims=True))
    corr = jnp.exp2(m_prev - m_new)
    p = jnp.exp2((s - m_new).astype(BF16))
    pv = None
    for u, vt in enumerate(vt_pieces):
        part = _dot(vt, p[u * PROJ_TM:(u + 1) * PROJ_TM, :])
        pv = part if pv is None else pv + part
    acc_ref[:, cols] = corr * acc_ref[:, cols] + pv
    m_ref[:, cols] = m_new


def _init_stats(m_ref, acc_ref):
    m_ref[...] = jnp.full(m_ref.shape, -jnp.inf, F32)
    acc_ref[...] = jnp.zeros(acc_ref.shape, F32)


def _normalized(acc_ref, dv):
    return acc_ref[0:dv, :] * (1.0 / acc_ref[dv:dv + 1, :])


def _mla_attn_kernel(qt_ref, k_ref, vt_ref, o_ref, m_ref, acc_ref, *s_refs, tq, tk):
    qi = pl.program_id(2)
    n_piece = tk // PROJ_TM
    n_group = tq // ATTN_GW
    n_full = qi * (tq // tk)
    _init_stats(m_ref, acc_ref)

    def score(unit, s_ref):
        c, g, _ = unit
        kc = k_ref[pl.ds(pl.multiple_of(c * tk, tk), tk), :]
        for half in range(GROUP_PIECES):
            s_ref[:, half * PROJ_TM:(half + 1) * PROJ_TM] = _dot(kc, qt_ref[g * GROUP_PIECES + half])

    def consume(unit, s):
        c, g, key_off = unit
        if key_off is not None:
            s = _causal_mask_t(s, g, key_off)
        vts = [vt_ref[c * n_piece + u] for u in range(n_piece)]
        _softmax_step_t(s, vts, m_ref, acc_ref, slice(g * ATTN_GW, (g + 1) * ATTN_GW))

    step = max(MLA_CHUNKS_PER_ITER, len(s_refs) // n_group)
    assert (step * n_group) % len(s_refs) == 0 and (tq // tk) % step == 0

    def chunk_units(c0):
        return [(c0 + dc, g, None) for dc in range(step) for g in range(n_group)]

    def body(i, carry):
        c0 = i * step
        _run_pipelined(chunk_units(c0), s_refs, score, consume, next_units=chunk_units(c0 + step))
        return carry

    _prime_pipeline(chunk_units(0), s_refs, score)
    lax.fori_loop(0, n_full // step, body, 0)
    diag = [(n_full + j, g, off) for j, g, off in _diag_units(tq, tk)]
    _run_pipelined(diag, s_refs, score, consume)
    o_ref[...] = _normalized(acc_ref, MLA_V_DIM).T.astype(o_ref.dtype)


def _mla_attn(qt, k, vt):
    b, s, _ = k.shape
    tq, tk = ATTN_TQ, ATTN_TK
    rows = MLA_V_DIM + ONES_ROWS
    return pl.pallas_call(
        functools.partial(_mla_attn_kernel, tq=tq, tk=tk),
        grid=(b, MLA_HEADS, s // tq),
        in_specs=[
            pl.BlockSpec((None, tq // PROJ_TM, None, MLA_QK_PAD, PROJ_TM),
                         lambda bi, h, i: (bi, i, h, 0, 0)),
            pl.BlockSpec((None, s, MLA_QK_PAD), lambda bi, h, i: (bi, 0, h)),
            pl.BlockSpec((None, s // PROJ_TM, None, rows, PROJ_TM),
                         lambda bi, h, i: (bi, 0, h, 0, 0)),
        ],
        out_specs=pl.BlockSpec((None, tq, MLA_V_DIM), lambda bi, h, i: (bi, i, h)),
        out_shape=jax.ShapeDtypeStruct((b, s, MLA_HEADS * MLA_V_DIM), BF16),
        scratch_shapes=[pltpu.VMEM((1, tq), F32), pltpu.VMEM((rows, tq), F32)]
        + [pltpu.VMEM((tk, ATTN_GW), F32)] * SCORE_RING,
        compiler_params=pltpu.CompilerParams(
            dimension_semantics=("parallel", "parallel", "arbitrary"),
            vmem_limit_bytes=VMEM_LIMIT_BYTES),
        name="mla_attn",
    )(qt, k, vt)


def _diff_attn_kernel(slopes_ref, qt_ref, k_ref, vt_ref, posq_ref, posk_ref,
                      lq1_ref, lk1_ref, lq2_ref, lk2_ref, sg_ref, o_ref,
                      m0_ref, acc0_ref, m1_ref, acc1_ref, *s_refs, tq, tk, lambda_init):
    h = pl.program_id(1)
    qi = pl.program_id(2)
    n_piece = tk // PROJ_TM
    n_group = tq // ATTN_GW
    n_full = qi * (tq // tk)
    slope = slopes_ref[h] * LOG2E
    stats = ((m0_ref, acc0_ref), (m1_ref, acc1_ref))
    for m_ref, acc_ref in stats:
        _init_stats(m_ref, acc_ref)

    def make_fns():
        bias_cache = {}

        def score(unit, s_ref):
            ckey, c, g, mi, _ = unit
            rows = pl.ds(pl.multiple_of(c * tk, tk), tk)
            if (ckey, g) not in bias_cache:
                posk = posk_ref[rows, :]
                posq = posq_ref[:, g * ATTN_GW:(g + 1) * ATTN_GW]
                bias_cache[(ckey, g)] = slope * jnp.abs(posk - posq)
            bias = bias_cache[(ckey, g)]
            kc = k_ref[rows, mi * DIFF_QK_DIM:(mi + 1) * DIFF_QK_DIM]
            for half in range(GROUP_PIECES):
                lanes = slice(half * PROJ_TM, (half + 1) * PROJ_TM)
                s_ref[:, lanes] = _dot(kc, qt_ref[g * GROUP_PIECES + half, mi]) - bias[:, lanes]

        def consume(unit, s):
            _, c, g, mi, key_off = unit
            if key_off is not None:
                s = _causal_mask_t(s, g, key_off)
            m_ref, acc_ref = stats[mi]
            vts = [vt_ref[c * n_piece + u] for u in range(n_piece)]
            _softmax_step_t(s, vts, m_ref, acc_ref, slice(g * ATTN_GW, (g + 1) * ATTN_GW))

        return score, consume

    def chunk_units(ckey, c):
        return [(ckey, c, g, mi, None) for g in range(n_group) for mi in range(2)]

    assert (2 * n_group) % len(s_refs) == 0

    def body(c, carry):
        score, consume = make_fns()
        _run_pipelined(chunk_units("cur", c), s_refs, score, consume,
                       next_units=chunk_units("next", c + 1))
        return carry

    score, consume = make_fns()
    _prime_pipeline(chunk_units("first", 0), s_refs, score)
    lax.fori_loop(0, n_full, body, 0)
    diag = [(j, n_full + j, g, mi, off) for j, g, off in _diag_units(tq, tk) for mi in range(2)]
    _run_pipelined(diag, s_refs, score, consume)

    lam = (jnp.exp(jnp.sum(lq1_ref[...] * lk1_ref[...], axis=-1, keepdims=True))
           - jnp.exp(jnp.sum(lq2_ref[...] * lk2_ref[...], axis=-1, keepdims=True))
           + lambda_init)
    o_t = _normalized(acc0_ref, DIFF_V_DIM) - lam * _normalized(acc1_ref, DIFF_V_DIM)
    o = _rms_norm_rows(o_t.T, sg_ref[...]) * (1.0 - lambda_init)
    o_ref[...] = o.astype(o_ref.dtype)


def _diff_attn(slopes, dqt, dk, dvt, positions, lq1, lk1, lq2, lk2, sg, *, lambda_init):
    b, s, _ = dk.shape
    tq, tk = ATTN_TQ, ATTN_TK
    hd = 2 * DIFF_QK_DIM
    rows = DIFF_V_DIM + ONES_ROWS
    posf = positions.astype(F32)
    posq = posf.reshape(b, 1, s)
    posk = posf.reshape(b, s, 1)
    vec = lambda n: pl.BlockSpec((1, n), lambda bi, h, i: (0, 0))
    return pl.pallas_call(
        functools.partial(_diff_attn_kernel, tq=tq, tk=tk, lambda_init=lambda_init),
        grid=(b, DIFF_HEADS, s // tq),
        in_specs=[
            pl.BlockSpec(memory_space=pltpu.SMEM),
            pl.BlockSpec((None, tq // PROJ_TM, None, 2, DIFF_QK_DIM, PROJ_TM),
                         lambda bi, h, i: (bi, i, h, 0, 0, 0)),
            pl.BlockSpec((None, s, hd), lambda bi, h, i: (bi, 0, h)),
            pl.BlockSpec((None, s // PROJ_TM, None, rows, PROJ_TM),
                         lambda bi, h, i: (bi, 0, h, 0, 0)),
            pl.BlockSpec((None, 1, tq), lambda bi, h, i: (bi, 0, i)),
            pl.BlockSpec((None, s, 1), lambda bi, h, i: (bi, 0, 0)),
            vec(DIFF_QK_DIM), vec(DIFF_QK_DIM), vec(DIFF_QK_DIM), vec(DIFF_QK_DIM),
            vec(DIFF_V_DIM),
        ],
        out_specs=pl.BlockSpec((None, tq, DIFF_V_DIM), lambda bi, h, i: (bi, i, h)),
        out_shape=jax.ShapeDtypeStruct((b, s, DIFF_V_COLS), BF16),
        scratch_shapes=[pltpu.VMEM((1, tq), F32), pltpu.VMEM((rows, tq), F32),
                        pltpu.VMEM((1, tq), F32), pltpu.VMEM((rows, tq), F32)]
        + [pltpu.VMEM((tk, ATTN_GW), F32)] * SCORE_RING,
        compiler_params=pltpu.CompilerParams(
            dimension_semantics=("parallel", "parallel", "arbitrary"),
            vmem_limit_bytes=VMEM_LIMIT_BYTES),
        name="diff_attn",
    )(slopes, dqt, dk, dvt, posq, posk, lq1, lk1, lq2, lk2, sg)


def _merge_ln_kernel(h_ref, om_ref, od_ref, wgm_ref, wgd_ref, wbm_ref, wbd_ref, wo_ref,
                     g_ref, b_ref, o_ref, y_ref, *, tn):
    h = h_ref[...]
    hb = h.astype(BF16)
    om = om_ref[...]
    od = od_ref[...]
    for c in range(D_MODEL // tn):
        cols = slice(c * tn, (c + 1) * tn)
        gm = jax.nn.sigmoid(_dot(hb, wgm_ref[:, cols]))
        gd = jax.nn.sigmoid(_dot(hb, wgd_ref[:, cols]))
        y = gm * _dot(om, wbm_ref[:, cols]) + gd * _dot(od, wbd_ref[:, cols])
        y_ref[:, cols] = y.astype(BF16)
    mix = _dot(y_ref[...], wo_ref[...])
    o_ref[...] = _layer_norm_rows(ALPHA * h + mix, g_ref[...], b_ref[...])


def _merge_ln(h, om, od, wgm, wgd, wbm, wbd, wo, g, b, *, tm=256, tn=512):
    t, d = h.shape
    row = lambda n: pl.BlockSpec((tm, n), lambda i: (i, 0))
    return pl.pallas_call(
        functools.partial(_merge_ln_kernel, tn=tn),
        grid=(t // tm,),
        in_specs=[row(d), row(om.shape[1]), row(od.shape[1]),
                  _resident(wgm.shape), _resident(wgd.shape), _resident(wbm.shape),
                  _resident(wbd.shape), _resident(wo.shape),
                  _resident(g.shape), _resident(b.shape)],
        out_specs=row(d),
        out_shape=jax.ShapeDtypeStruct((t, d), F32),
        scratch_shapes=[pltpu.VMEM((tm, d), BF16)],
        compiler_params=pltpu.CompilerParams(
            dimension_semantics=("parallel",), vmem_limit_bytes=VMEM_LIMIT_BYTES),
        name="merge_ln",
    )(h, om, od, wgm, wgd, wbm, wbd, wo, g, b)


def _rotate_half_cols(w):
    half = w.shape[-1] // 2
    return jnp.concatenate([-w[..., half:], w[..., :half]], axis=-1)


def _prep_mixer_weights(w_in, w_uq):
    d = w_in.shape[0]
    base = MLA_Q_RANK + MLA_KV_RANK
    w_kr = w_in[:, base:base + MLA_ROPE_DIM]
    zpad = jnp.zeros((d, LANE - MLA_ROPE_DIM), w_in.dtype)
    wlat = jnp.concatenate([w_in[:, :base], w_kr, zpad, _rotate_half_cols(w_kr), zpad], axis=1)
    d0 = base + MLA_ROPE_DIM
    wdqt = w_in[:, d0:d0 + DIFF_QK_COLS].T
    wdk = w_in[:, d0 + DIFF_QK_COLS:d0 + 2 * DIFF_QK_COLS]
    v0 = d0 + 2 * DIFF_QK_COLS
    wdvt = w_in[:, v0:v0 + DIFF_V_COLS].T
    g0 = v0 + DIFF_V_COLS
    wgm = w_in[:, g0:g0 + D_MODEL]
    wgd = w_in[:, g0 + D_MODEL:g0 + 2 * D_MODEL]

    r = w_uq.shape[0]
    wq3 = w_uq.reshape(r, MLA_HEADS, MLA_QK_DIM)
    nope, rope = wq3[..., :MLA_NOPE_DIM], wq3[..., MLA_NOPE_DIM:]
    z3 = jnp.zeros((r, MLA_HEADS, LANE - MLA_ROPE_DIM), w_uq.dtype)
    wq = jnp.concatenate([nope, rope, z3], axis=-1).reshape(r, MLA_HEADS * MLA_QK_PAD)
    wqp = jnp.concatenate([_rotate_half_cols(rope), z3], axis=-1).reshape(r, MLA_HEADS * LANE)
    return wlat, wdqt, wdk, wdvt, wgm, wgd, wq.T, wqp.T


def kernel(x, positions, ln1_g, ln1_b, ffn1_w_gate, ffn1_w_up, ffn1_w_down, w_in, mla_q_norm_g, mla_w_uq, mla_kv_norm_g, mla_w_uk, mla_w_uv, diff_lambda_q1, diff_lambda_k1, diff_lambda_q2, diff_lambda_k2, diff_subln_g, w_branch_mla, w_branch_diff, w_out, ln2_g, ln2_b, ffn2_w_gate, ffn2_w_up, ffn2_w_down, ln3_g, ln3_b):
    b, s, d = x.shape
    t = b * s
    bf = lambda w: w.astype(BF16)

    half = MLA_ROPE_DIM // 2
    inv_freq = ROPE_THETA ** (-jnp.arange(half, dtype=F32) / half)
    invf = inv_freq[:, None]
    slopes = 2.0 ** (-8.0 * jnp.arange(1, DIFF_HEADS + 1, dtype=F32) / DIFF_HEADS)

    h = x.reshape(t, d)
    for l in range(DEPTH):
        lambda_init = 0.8 - 0.6 * math.exp(-0.3 * l)
        h = _ffn_ln(h, bf(ffn1_w_gate[l]), bf(ffn1_w_up[l]), bf(ffn1_w_down[l]),
                    ln1_g[l][None, :], ln1_b[l][None, :])

        wlat, wdqt, wdk, wdvt, wgm, wgd, wqt, wqpt = _prep_mixer_weights(
            bf(w_in[l]), bf(mla_w_uq[l]))
        pieces = s // PROJ_TM
        qt, k, vt, dqt, dk, dvt = _mixer_proj(
            h, positions.reshape(t // PROJ_TM, 1, PROJ_TM).astype(F32), invf,
            wlat, wdqt, wdk, wdvt, mla_q_norm_g[l][None, :], mla_kv_norm_g[l][None, :],
            wqt, wqpt, bf(mla_w_uk[l]), bf(mla_w_uv[l]).T)

        o_mla = _mla_attn(qt.reshape((b, pieces) + qt.shape[1:]), k.reshape(b, s, -1),
                          vt.reshape((b, pieces) + vt.shape[1:]))
        o_diff = _diff_attn(
            slopes, dqt.reshape(b, pieces, DIFF_HEADS, 2, DIFF_QK_DIM, PROJ_TM),
            dk.reshape(b, s, -1), dvt.reshape((b, pieces) + dvt.shape[1:]), positions,
            diff_lambda_q1[l][None, :], diff_lambda_k1[l][None, :],
            diff_lambda_q2[l][None, :], diff_lambda_k2[l][None, :],
            diff_subln_g[l][None, :], lambda_init=lambda_init)

        h = _merge_ln(h, o_mla.reshape(t, -1), o_diff.reshape(t, -1), wgm, wgd,
                      bf(w_branch_mla[l]), bf(w_branch_diff[l]), bf(w_out[l]),
                      ln2_g[l][None, :], ln2_b[l][None, :])

        h = _ffn_ln(h, bf(ffn2_w_gate[l]), bf(ffn2_w_up[l]), bf(ffn2_w_down[l]),
                    ln3_g[l][None, :], ln3_b[l][None, :])
    return h.reshape(b, s, d)
```

```python
import functools
import math

import jax
import jax.numpy as jnp
from jax import lax
from jax.experimental import pallas as pl
from jax.experimental.pallas import tpu as pltpu

D_MODEL = 2048
DEPTH = 1
MLA_HEADS = 8
MLA_Q_RANK = 512
MLA_KV_RANK = 512
MLA_NOPE_DIM = 128
MLA_ROPE_DIM = 64
MLA_V_DIM = 128
MLA_QK_DIM = MLA_NOPE_DIM + MLA_ROPE_DIM
MLA_QK_PAD = 256
ROPE_THETA = 10000.0
DIFF_HEADS = 4
DIFF_QK_DIM = 128
DIFF_V_DIM = 2 * DIFF_QK_DIM
DIFF_QK_COLS = DIFF_HEADS * 2 * DIFF_QK_DIM
DIFF_V_COLS = DIFF_HEADS * DIFF_V_DIM
D_FF = 5632
LN_EPS = 1e-5
RMS_EPS = 1e-6
ALPHA = (2 * DEPTH) ** 0.25
LOG2E = math.log2(math.e)

LANE = 128
MXU_DIM = 256
ONES_ROWS = 16
VMEM_LIMIT_BYTES = 56 * 1024 * 1024
VMEM_INTERNAL_SCRATCH_BYTES = 8 * 1024 * 1024
PROJ_TM = MXU_DIM
ATTN_TQ = 2048
ATTN_TK = 512
ATTN_GW = 2 * MXU_DIM
GROUP_PIECES = ATTN_GW // PROJ_TM
LOOKAHEAD = 2
SCORE_RING = 4
MLA_CHUNKS_PER_ITER = 1

F32 = jnp.float32
BF16 = jnp.bfloat16


def _dot(a, b):
    return jnp.dot(a, b, preferred_element_type=F32)


def _dot_nt(a, b):
    return lax.dot_general(a, b, (((1,), (1,)), ((), ())), preferred_element_type=F32)


def _layer_norm_rows(y, g, b):
    mu = jnp.mean(y, axis=-1, keepdims=True)
    d = y - mu
    var = jnp.mean(d * d, axis=-1, keepdims=True)
    return d * lax.rsqrt(var + LN_EPS) * g + b


def _rms_norm_rows(y, g):
    ms = jnp.mean(y * y, axis=-1, keepdims=True)
    return y * lax.rsqrt(ms + RMS_EPS) * g


def _resident(shape):
    return pl.BlockSpec(shape, lambda *_: (0,) * len(shape), pipeline_mode=pl.Buffered(1))


def _cast_slab_spec(w, n_steps, step_of=lambda *idx: idx[0]):
    rows, cols = w.shape
    slabs = n_steps
    while rows % slabs or (rows // slabs) % 16:
        slabs //= 2
    per = n_steps // slabs
    return pl.BlockSpec((rows // slabs, cols), lambda *idx: (step_of(*idx) // per, 0))


def _cast_slabs(refs):
    n = len(refs) // 2
    for src, dst in zip(refs[:n], refs[n:]):
        dst[...] = src[...].astype(BF16)


def _ffn_step(x_ref, wg_ref, wu_ref, wd_ref, g_ref, b_ref, o_ref, xb_ref):
    j = pl.program_id(1)

    @pl.when(j == 0)
    def _():
        x = x_ref[...]
        xb_ref[...] = x.astype(BF16)
        o_ref[...] = ALPHA * x

    xb = xb_ref[...]
    gate = _dot(xb, wg_ref[...])
    up = _dot(xb, wu_ref[...])
    act = (0.5 * (gate * jax.nn.sigmoid(gate))) * up
    o_ref[...] += _dot(act.astype(BF16), wd_ref[...])

    @pl.when(j == pl.num_programs(1) - 1)
    def _():
        o_ref[...] = _layer_norm_rows(o_ref[...], g_ref[...], b_ref[...])


def _ffn_ln_kernel(x_ref, wg_ref, wu_ref, wd_ref, g_ref, b_ref, *rest):
    o_ref, xb_ref = rest[-2:]
    _ffn_step(x_ref, wg_ref, wu_ref, wd_ref, g_ref, b_ref, o_ref, xb_ref)


def _ffn_ln_head_kernel(x_ref, wg_ref, wu_ref, wd_ref, g_ref, b_ref,
                        o_ref, wgb_ref, wub_ref, wdb_ref, xb_ref):
    _cast_slabs((wg_ref, wu_ref, wd_ref, wgb_ref, wub_ref, wdb_ref))
    _ffn_step(x_ref, wgb_ref, wub_ref, wdb_ref, g_ref, b_ref, o_ref, xb_ref)


def _ffn_specs(tm, tf, d, tile_off):
    return [
        pl.BlockSpec((tm, d), lambda i, j: (i + tile_off, 0)),
        pl.BlockSpec((d, tf), lambda i, j: (0, j)),
        pl.BlockSpec((d, tf), lambda i, j: (0, j)),
        pl.BlockSpec((tf, d), lambda i, j: (j, 0)),
        pl.BlockSpec((1, d), lambda i, j: (0, 0)),
        pl.BlockSpec((1, d), lambda i, j: (0, 0)),
    ]


def _ffn_ln_head(x, wg, wu, wd, g, b, *, tm=1024, tf=256):
    t, d = x.shape
    f = wg.shape[1]
    specs = _ffn_specs(tm, tf, d, 0)
    vmem_bytes = ((2 * tm * d * 4) * 2 + tm * d * 2 + 2 * (3 * d * tf * (4 + 2))
                  + tm * tf * (4 + 4 + 2))
    return pl.pallas_call(
        _ffn_ln_head_kernel,
        grid=(1, f // tf),
        in_specs=specs,
        out_specs=[specs[0]] + specs[1:4],
        out_shape=[jax.ShapeDtypeStruct((t, d), F32)]
        + [jax.ShapeDtypeStruct(w.shape, BF16) for w in (wg, wu, wd)],
        scratch_shapes=[pltpu.VMEM((tm, d), BF16)],
        compiler_params=pltpu.CompilerParams(
            dimension_semantics=("arbitrary", "arbitrary"),
            vmem_limit_bytes=vmem_bytes + VMEM_INTERNAL_SCRATCH_BYTES),
        name="ffn_ln_head",
    )(x, wg, wu, wd, g, b)


def _ffn_ln(x, wg, wu, wd, g, b, *, head=None, tm=1024, tf=512):
    t, d = x.shape
    f = wg.shape[1]
    tile_off = 0 if head is None else 1
    specs = _ffn_specs(tm, tf, d, tile_off)
    extra = () if head is None else (head,)
    vmem_bytes = (2 * tm * d * 4) * 2 + tm * d * 2 + 2 * (3 * d * tf * 2) + tm * tf * (4 + 4 + 2)
    return pl.pallas_call(
        _ffn_ln_kernel,
        grid=(t // tm - tile_off, f // tf),
        in_specs=specs + [pl.BlockSpec(memory_space=pl.ANY)] * len(extra),
        out_specs=specs[0],
        out_shape=jax.ShapeDtypeStruct((t, d), F32),
        input_output_aliases={len(specs): 0} if extra else {},
        scratch_shapes=[pltpu.VMEM((tm, d), BF16)],
        compiler_params=pltpu.CompilerParams(
            dimension_semantics=("parallel", "arbitrary"),
            vmem_limit_bytes=vmem_bytes + VMEM_INTERNAL_SCRATCH_BYTES),
        name="ffn_ln",
    )(x, wg, wu, wd, g, b, *extra)


def _mixer_proj_kernel(h_ref, pos_ref, invf_ref, wlat_ref, wdqt_ref, wdk_ref, wdvt_ref,
                       qg_ref, kvg_ref, wqt_ref, wqpt_ref, wuk_ref, wuvt_ref,
                       qt_ref, k_ref, vt_ref, dqt_ref, dk_ref, dvt_ref):
    hb = h_ref[...].astype(BF16)
    tm = hb.shape[0]

    dq_scale = DIFF_QK_DIM ** -0.5 * LOG2E
    dqt_ref[...] = (_dot_nt(wdqt_ref[...], hb) * dq_scale).astype(BF16)
    dk_ref[...] = _dot(hb, wdk_ref[...]).astype(BF16)
    ones_rows = (lax.broadcasted_iota(jnp.int32, (ONES_ROWS, tm), 0) == 0).astype(BF16)
    dvt = _dot_nt(wdvt_ref[...], hb).astype(BF16)
    for h in range(DIFF_HEADS):
        dvt_ref[h, 0:DIFF_V_DIM, :] = dvt[h * DIFF_V_DIM:(h + 1) * DIFF_V_DIM, :]
        dvt_ref[h, DIFF_V_DIM:, :] = ones_rows

    lat = _dot(hb, wlat_ref[...])
    c_q = lat[:, 0:MLA_Q_RANK]
    c_kv = lat[:, MLA_Q_RANK:MLA_Q_RANK + MLA_KV_RANK]
    base = MLA_Q_RANK + MLA_KV_RANK
    k_r = lat[:, base:base + LANE]
    k_r_rot = lat[:, base + LANE:base + 2 * LANE]

    ang = invf_ref[...] * pos_ref[...]
    cos_f, sin_f = jnp.cos(ang), jnp.sin(ang)
    pad = jnp.zeros((LANE - MLA_ROPE_DIM, tm), F32)
    cos_t = jnp.concatenate([cos_f, cos_f, pad], axis=0)
    sin_t = jnp.concatenate([sin_f, sin_f, pad], axis=0)
    cos, sin = cos_t.T, sin_t.T

    cqn = _rms_norm_rows(c_q, qg_ref[...]).astype(BF16)
    ckvn = _rms_norm_rows(c_kv, kvg_ref[...]).astype(BF16)

    q_t = _dot_nt(wqt_ref[...], cqn)
    qp_t = _dot_nt(wqpt_ref[...], cqn)
    k_nope = _dot(ckvn, wuk_ref[...])
    vt = _dot_nt(wuvt_ref[...], ckvn).astype(BF16)
    for h in range(MLA_HEADS):
        vt_ref[h, 0:MLA_V_DIM, :] = vt[h * MLA_V_DIM:(h + 1) * MLA_V_DIM, :]
        vt_ref[h, MLA_V_DIM:, :] = ones_rows

    q_scale = MLA_QK_DIM ** -0.5 * LOG2E
    k_rope = (k_r * cos + k_r_rot * sin).astype(BF16)
    for h in range(MLA_HEADS):
        c0 = h * MLA_QK_PAD
        qt_ref[h, 0:LANE, :] = (q_t[c0:c0 + LANE, :] * q_scale).astype(BF16)
        q_rope = q_t[c0 + LANE:c0 + 2 * LANE, :] * cos_t + qp_t[h * LANE:(h + 1) * LANE, :] * sin_t
        qt_ref[h, LANE:2 * LANE, :] = (q_rope * q_scale).astype(BF16)
        k_ref[:, c0:c0 + LANE] = k_nope[:, h * LANE:(h + 1) * LANE].astype(BF16)
        k_ref[:, c0 + LANE:c0 + 2 * LANE] = k_rope


def _mixer_proj(h, pos, invf, wlat, wdqt, wdk, wdvt, qg, kvg, wqt, wqpt, wuk, wuvt):
    t, d = h.shape
    tm = PROJ_TM
    qk_cols = MLA_HEADS * MLA_QK_PAD
    row = lambda n: pl.BlockSpec((tm, n), lambda i: (i, 0))
    tile_t = lambda *dims: pl.BlockSpec((None,) + dims + (tm,),
                                        lambda i: (i,) + (0,) * (len(dims) + 1))
    rows_out = lambda n: jax.ShapeDtypeStruct((t, n), BF16)
    tile_t_out = lambda *dims: jax.ShapeDtypeStruct((t // tm,) + dims + (tm,), BF16)
    weights = (invf, wlat, wdqt, wdk, wdvt, qg, kvg, wqt, wqpt, wuk, wuvt)
    return pl.pallas_call(
        _mixer_proj_kernel,
        grid=(t // tm,),
        in_specs=[row(d), tile_t(1)] + [_resident(w.shape) for w in weights],
        out_specs=[tile_t(MLA_HEADS, MLA_QK_PAD), row(qk_cols),
                   tile_t(MLA_HEADS, MLA_V_DIM + ONES_ROWS),
                   tile_t(DIFF_QK_COLS), row(DIFF_QK_COLS),
                   tile_t(DIFF_HEADS, DIFF_V_DIM + ONES_ROWS)],
        out_shape=[tile_t_out(MLA_HEADS, MLA_QK_PAD), rows_out(qk_cols),
                   tile_t_out(MLA_HEADS, MLA_V_DIM + ONES_ROWS),
                   tile_t_out(DIFF_QK_COLS), rows_out(DIFF_QK_COLS),
                   tile_t_out(DIFF_HEADS, DIFF_V_DIM + ONES_ROWS)],
        compiler_params=pltpu.CompilerParams(
            dimension_semantics=("parallel",), vmem_limit_bytes=VMEM_LIMIT_BYTES),
        name="mixer_proj",
    )(h, pos, *weights)


def _diag_units(tq, tk):
    units = []
    for j in range(tq // tk):
        for g in range(tq // ATTN_GW):
            if j * tk >= (g + 1) * ATTN_GW:
                continue
            visible = (j + 1) * tk <= g * ATTN_GW + 1
            units.append((j, g, None if visible else j * tk))
    return units


def _causal_mask_t(s, g, key_off):
    rows = lax.broadcasted_iota(jnp.int32, s.shape, 0)
    cols = lax.broadcasted_iota(jnp.int32, s.shape, 1)
    return jnp.where(rows + key_off <= cols + g * ATTN_GW, s, -jnp.inf)


def _run_pipelined(units, s_refs, score_fn, consume_fn, next_units=()):
    ring = len(s_refs)
    todo = list(enumerate(list(units) + list(next_units)))[LOOKAHEAD:]
    for k, u in enumerate(units):
        if todo:
            kk, nu = todo.pop(0)
            score_fn(nu, s_refs[kk % ring])
        consume_fn(u, s_refs[k % ring][...])


def _prime_pipeline(units, s_refs, score_fn):
    for k, u in enumerate(units[:LOOKAHEAD]):
        score_fn(u, s_refs[k])


def _softmax_step_t(s, vt_pieces, m_ref, acc_ref, cols):
    m_prev = m_ref[:, cols]
    m_new = jnp.maximum(m_prev, jnp.max(s, axis=0, keepdims=True))
    corr = jnp.exp2(m_prev - m_new)
    p = jnp.exp2((s - m_new).astype(BF16))
    pv = None
    for u, vt in enumerate(vt_pieces):
        part = _dot(vt, p[u * PROJ_TM:(u + 1) * PROJ_TM, :])
        pv = part if pv is None else pv + part
    acc_ref[:, cols] = corr * acc_ref[:, cols] + pv
    m_ref[:, cols] = m_new


def _init_stats(m_ref, acc_ref):
    m_ref[...] = jnp.full(m_ref.shape, -jnp.inf, F32)
    acc_ref[...] = jnp.zeros(acc_ref.shape, F32)


def _normalized(acc_ref, dv):
    return acc_ref[0:dv, :] * (1.0 / acc_ref[dv:dv + 1, :])


def _mla_attn_kernel(qt_ref, k_ref, vt_ref, *rest, tq, tk, n_cast):
    cast_in, o_ref, cast_out = rest[:n_cast], rest[n_cast], rest[n_cast + 1:2 * n_cast + 1]
    m_ref, acc_ref, *s_refs = rest[2 * n_cast + 1:]
    _cast_slabs(cast_in + cast_out)
    qi = pl.program_id(2)
    n_piece = tk // PROJ_TM
    n_group = tq // ATTN_GW
    n_full = qi * (tq // tk)
    _init_stats(m_ref, acc_ref)

    def score(unit, s_ref):
        c, g, _ = unit
        kc = k_ref[pl.ds(pl.multiple_of(c * tk, tk), tk), :]
        for half in range(GROUP_PIECES):
            s_ref[:, half * PROJ_TM:(half + 1) * PROJ_TM] = _dot(kc, qt_ref[g * GROUP_PIECES + half])

    def consume(unit, s):
        c, g, key_off = unit
        if key_off is not None:
            s = _causal_mask_t(s, g, key_off)
        vts = [vt_ref[c * n_piece + u] for u in range(n_piece)]
        _softmax_step_t(s, vts, m_ref, acc_ref, slice(g * ATTN_GW, (g + 1) * ATTN_GW))

    step = max(MLA_CHUNKS_PER_ITER, len(s_refs) // n_group)
    assert (step * n_group) % len(s_refs) == 0 and (tq // tk) % step == 0

    def chunk_units(c0):
        return [(c0 + dc, g, None) for dc in range(step) for g in range(n_group)]

    def body(i, carry):
        c0 = i * step
        _run_pipelined(chunk_units(c0), s_refs, score, consume, next_units=chunk_units(c0 + step))
        return carry

    _prime_pipeline(chunk_units(0), s_refs, score)
    lax.fori_loop(0, n_full // step, body, 0)
    diag = [(n_full + j, g, off) for j, g, off in _diag_units(tq, tk)]
    _run_pipelined(diag, s_refs, score, consume)
    o_ref[...] = _normalized(acc_ref, MLA_V_DIM).T.astype(o_ref.dtype)


def _mla_attn(qt, k, vt, cast_weights):
    b, s, _ = k.shape
    tq, tk = ATTN_TQ, ATTN_TK
    rows = MLA_V_DIM + ONES_ROWS
    nq = s // tq
    steps = b * MLA_HEADS * nq
    cast_specs = [_cast_slab_spec(w, steps, lambda bi, h, i: (bi * MLA_HEADS + h) * nq + i)
                  for w in cast_weights]
    outs = pl.pallas_call(
        functools.partial(_mla_attn_kernel, tq=tq, tk=tk, n_cast=len(cast_weights)),
        grid=(b, MLA_HEADS, nq),
        in_specs=[
            pl.BlockSpec((None, tq // PROJ_TM, None, MLA_QK_PAD, PROJ_TM),
                         lambda bi, h, i: (bi, i, h, 0, 0)),
            pl.BlockSpec((None, s, MLA_QK_PAD), lambda bi, h, i: (bi, 0, h)),
            pl.BlockSpec((None, s // PROJ_TM, None, rows, PROJ_TM),
                         lambda bi, h, i: (bi, 0, h, 0, 0)),
        ] + cast_specs,
        out_specs=[pl.BlockSpec((None, tq, MLA_V_DIM), lambda bi, h, i: (bi, i, h))] + cast_specs,
        out_shape=[jax.ShapeDtypeStruct((b, s, MLA_HEADS * MLA_V_DIM), BF16)]
        + [jax.ShapeDtypeStruct(w.shape, BF16) for w in cast_weights],
        scratch_shapes=[pltpu.VMEM((1, tq), F32), pltpu.VMEM((rows, tq), F32)]
        + [pltpu.VMEM((tk, ATTN_GW), F32)] * SCORE_RING,
        compiler_params=pltpu.CompilerParams(
            dimension_semantics=("arbitrary", "arbitrary", "arbitrary"),
            vmem_limit_bytes=VMEM_LIMIT_BYTES),
        name="mla_attn",
    )(qt, k, vt, *cast_weights)
    return outs[0], outs[1:]


def _diff_attn_kernel(slopes_ref, qt_ref, k_ref, vt_ref, posq_ref, posk_ref,
                      lq1_ref, lk1_ref, lq2_ref, lk2_ref, sg_ref, o_ref,
                      m0_ref, acc0_ref, m1_ref, acc1_ref, *s_refs, tq, tk, lambda_init):
    h = pl.program_id(1)
    qi = pl.program_id(2)
    n_piece = tk // PROJ_TM
    n_group = tq // ATTN_GW
    n_full = qi * (tq // tk)
    slope = slopes_ref[h] * LOG2E
    stats = ((m0_ref, acc0_ref), (m1_ref, acc1_ref))
    for m_ref, acc_ref in stats:
        _init_stats(m_ref, acc_ref)

    def make_fns():
        bias_cache = {}

        def score(unit, s_ref):
            ckey, c, g, mi, _ = unit
            rows = pl.ds(pl.multiple_of(c * tk, tk), tk)
            if (ckey, g) not in bias_cache:
                posk = posk_ref[rows, :]
                posq = posq_ref[:, g * ATTN_GW:(g + 1) * ATTN_GW]
                bias_cache[(ckey, g)] = slope * jnp.abs(posk - posq)
            bias = bias_cache[(ckey, g)]
            kc = k_ref[rows, mi * DIFF_QK_DIM:(mi + 1) * DIFF_QK_DIM]
            for half in range(GROUP_PIECES):
                lanes = slice(half * PROJ_TM, (half + 1) * PROJ_TM)
                s_ref[:, lanes] = _dot(kc, qt_ref[g * GROUP_PIECES + half, mi]) - bias[:, lanes]

        def consume(unit, s):
            _, c, g, mi, key_off = unit
            if key_off is not None:
                s = _causal_mask_t(s, g, key_off)
            m_ref, acc_ref = stats[mi]
            vts = [vt_ref[c * n_piece + u] for u in range(n_piece)]
            _softmax_step_t(s, vts, m_ref, acc_ref, slice(g * ATTN_GW, (g + 1) * ATTN_GW))

        return score, consume

    def chunk_units(ckey, c):
        return [(ckey, c, g, mi, None) for g in range(n_group) for mi in range(2)]

    assert (2 * n_group) % len(s_refs) == 0

    def body(c, carry):
        score, consume = make_fns()
        _run_pipelined(chunk_units("cur", c), s_refs, score, consume,
                       next_units=chunk_units("next", c + 1))
        return carry

    score, consume = make_fns()
    _prime_pipeline(chunk_units("first", 0), s_refs, score)
    lax.fori_loop(0, n_full, body, 0)
    diag = [(j, n_full + j, g, mi, off) for j, g, off in _diag_units(tq, tk) for mi in range(2)]
    _run_pipelined(diag, s_refs, score, consume)

    lam = (jnp.exp(jnp.sum(lq1_ref[...] * lk1_ref[...], axis=-1, keepdims=True))
           - jnp.exp(jnp.sum(lq2_ref[...] * lk2_ref[...], axis=-1, keepdims=True))
           + lambda_init)
    o_t = _normalized(acc0_ref, DIFF_V_DIM) - lam * _normalized(acc1_ref, DIFF_V_DIM)
    o = _rms_norm_rows(o_t.T, sg_ref[...]) * (1.0 - lambda_init)
    o_ref[...] = o.astype(o_ref.dtype)


def _diff_attn(slopes, dqt, dk, dvt, positions, lq1, lk1, lq2, lk2, sg, *, lambda_init):
    b, s, _ = dk.shape
    tq, tk = ATTN_TQ, ATTN_TK
    hd = 2 * DIFF_QK_DIM
    rows = DIFF_V_DIM + ONES_ROWS
    posf = positions.astype(F32)
    posq = posf.reshape(b, 1, s)
    posk = posf.reshape(b, s, 1)
    vec = lambda n: pl.BlockSpec((1, n), lambda bi, h, i: (0, 0))
    return pl.pallas_call(
        functools.partial(_diff_attn_kernel, tq=tq, tk=tk, lambda_init=lambda_init),
        grid=(b, DIFF_HEADS, s // tq),
        in_specs=[
            pl.BlockSpec(memory_space=pltpu.SMEM),
            pl.BlockSpec((None, tq // PROJ_TM, None, 2, DIFF_QK_DIM, PROJ_TM),
                         lambda bi, h, i: (bi, i, h, 0, 0, 0)),
            pl.BlockSpec((None, s, hd), lambda bi, h, i: (bi, 0, h)),
            pl.BlockSpec((None, s // PROJ_TM, None, rows, PROJ_TM),
                         lambda bi, h, i: (bi, 0, h, 0, 0)),
            pl.BlockSpec((None, 1, tq), lambda bi, h, i: (bi, 0, i)),
            pl.BlockSpec((None, s, 1), lambda bi, h, i: (bi, 0, 0)),
            vec(DIFF_QK_DIM), vec(DIFF_QK_DIM), vec(DIFF_QK_DIM), vec(DIFF_QK_DIM),
            vec(DIFF_V_DIM),
        ],
        out_specs=pl.BlockSpec((None, tq, DIFF_V_DIM), lambda bi, h, i: (bi, i, h)),
        out_shape=jax.ShapeDtypeStruct((b, s, DIFF_V_COLS), BF16),
        scratch_shapes=[pltpu.VMEM((1, tq), F32), pltpu.VMEM((rows, tq), F32),
                        pltpu.VMEM((1, tq), F32), pltpu.VMEM((rows, tq), F32)]
        + [pltpu.VMEM((tk, ATTN_GW), F32)] * SCORE_RING,
        compiler_params=pltpu.CompilerParams(
            dimension_semantics=("parallel", "parallel", "arbitrary"),
            vmem_limit_bytes=VMEM_LIMIT_BYTES),
        name="diff_attn",
    )(slopes, dqt, dk, dvt, posq, posk, lq1, lk1, lq2, lk2, sg)


def _merge_ln_kernel(h_ref, om_ref, od_ref, wgm_ref, wgd_ref, wbm_ref, wbd_ref, wo_ref,
                     g_ref, b_ref, *rest, tn):
    n_cast = (len(rest) - 2) // 2
    o_ref, y_ref = rest[n_cast], rest[-1]
    _cast_slabs(rest[:n_cast] + rest[n_cast + 1:-1])
    h = h_ref[...]
    hb = h.astype(BF16)
    om = om_ref[...]
    od = od_ref[...]
    for c in range(D_MODEL // tn):
        cols = slice(c * tn, (c + 1) * tn)
        gm = jax.nn.sigmoid(_dot(hb, wgm_ref[:, cols]))
        gd = jax.nn.sigmoid(_dot(hb, wgd_ref[:, cols]))
        y = gm * _dot(om, wbm_ref[:, cols]) + gd * _dot(od, wbd_ref[:, cols])
        y_ref[:, cols] = y.astype(BF16)
    mix = _dot(y_ref[...], wo_ref[...])
    o_ref[...] = _layer_norm_rows(ALPHA * h + mix, g_ref[...], b_ref[...])


def _merge_ln(h, om, od, wgm, wgd, wbm, wbd, wo, g, b, cast_weights, *, tm=256, tn=512):
    t, d = h.shape
    steps = t // tm
    row = lambda n: pl.BlockSpec((tm, n), lambda i: (i, 0))
    cast_specs = [_cast_slab_spec(w, steps) for w in cast_weights]
    outs = pl.pallas_call(
        functools.partial(_merge_ln_kernel, tn=tn),
        grid=(steps,),
        in_specs=[row(d), row(om.shape[1]), row(od.shape[1]),
                  _resident(wgm.shape), _resident(wgd.shape), _resident(wbm.shape),
                  _resident(wbd.shape), _resident(wo.shape),
                  _resident(g.shape), _resident(b.shape)] + cast_specs,
        out_specs=[row(d)] + cast_specs,
        out_shape=[jax.ShapeDtypeStruct((t, d), F32)]
        + [jax.ShapeDtypeStruct(w.shape, BF16) for w in cast_weights],
        scratch_shapes=[pltpu.VMEM((tm, d), BF16)],
        compiler_params=pltpu.CompilerParams(
            dimension_semantics=("arbitrary",), vmem_limit_bytes=VMEM_LIMIT_BYTES),
        name="merge_ln",
    )(h, om, od, wgm, wgd, wbm, wbd, wo, g, b, *cast_weights)
    return outs[0], outs[1:]


def _rotate_half_cols(w):
    half = w.shape[-1] // 2
    return jnp.concatenate([-w[..., half:], w[..., :half]], axis=-1)


def _prep_mixer_weights(w_in, w_uq):
    d = w_in.shape[0]
    base = MLA_Q_RANK + MLA_KV_RANK
    w_kr = w_in[:, base:base + MLA_ROPE_DIM]
    zpad = jnp.zeros((d, LANE - MLA_ROPE_DIM), w_in.dtype)
    wlat = jnp.concatenate([w_in[:, :base], w_kr, zpad, _rotate_half_cols(w_kr), zpad], axis=1)
    d0 = base + MLA_ROPE_DIM
    wdqt = w_in[:, d0:d0 + DIFF_QK_COLS].T
    wdk = w_in[:, d0 + DIFF_QK_COLS:d0 + 2 * DIFF_QK_COLS]
    v0 = d0 + 2 * DIFF_QK_COLS
    wdvt = w_in[:, v0:v0 + DIFF_V_COLS].T
    g0 = v0 + DIFF_V_COLS
    wgm = w_in[:, g0:g0 + D_MODEL]
    wgd = w_in[:, g0 + D_MODEL:g0 + 2 * D_MODEL]

    r = w_uq.shape[0]
    wq3 = w_uq.reshape(r, MLA_HEADS, MLA_QK_DIM)
    nope, rope = wq3[..., :MLA_NOPE_DIM], wq3[..., MLA_NOPE_DIM:]
    z3 = jnp.zeros((r, MLA_HEADS, LANE - MLA_ROPE_DIM), w_uq.dtype)
    wq = jnp.concatenate([nope, rope, z3], axis=-1).reshape(r, MLA_HEADS * MLA_QK_PAD)
    wqp = jnp.concatenate([_rotate_half_cols(rope), z3], axis=-1).reshape(r, MLA_HEADS * LANE)
    return wlat, wdqt, wdk, wdvt, wgm, wgd, wq.T, wqp.T


def kernel(x, positions, ln1_g, ln1_b, ffn1_w_gate, ffn1_w_up, ffn1_w_down, w_in, mla_q_norm_g, mla_w_uq, mla_kv_norm_g, mla_w_uk, mla_w_uv, diff_lambda_q1, diff_lambda_k1, diff_lambda_q2, diff_lambda_k2, diff_subln_g, w_branch_mla, w_branch_diff, w_out, ln2_g, ln2_b, ffn2_w_gate, ffn2_w_up, ffn2_w_down, ln3_g, ln3_b):
    b, s, d = x.shape
    t = b * s
    bf = lambda w: w.astype(BF16)

    half = MLA_ROPE_DIM // 2
    inv_freq = ROPE_THETA ** (-jnp.arange(half, dtype=F32) / half)
    invf = inv_freq[:, None]
    slopes = 2.0 ** (-8.0 * jnp.arange(1, DIFF_HEADS + 1, dtype=F32) / DIFF_HEADS)

    h = x.reshape(t, d)
    for l in range(DEPTH):
        lambda_init = 0.8 - 0.6 * math.exp(-0.3 * l)
        head, w1g, w1u, w1d = _ffn_ln_head(h, ffn1_w_gate[l], ffn1_w_up[l], ffn1_w_down[l],
                                           ln1_g[l][None, :], ln1_b[l][None, :])
        h = _ffn_ln(h, w1g, w1u, w1d, ln1_g[l][None, :], ln1_b[l][None, :], head=head)

        wlat, wdqt, wdk, wdvt, wgm, wgd, wqt, wqpt = _prep_mixer_weights(
            bf(w_in[l]), bf(mla_w_uq[l]))
        pieces = s // PROJ_TM
        qt, k, vt, dqt, dk, dvt = _mixer_proj(
            h, positions.reshape(t // PROJ_TM, 1, PROJ_TM).astype(F32), invf,
            wlat, wdqt, wdk, wdvt, mla_q_norm_g[l][None, :], mla_kv_norm_g[l][None, :],
            wqt, wqpt, bf(mla_w_uk[l]), bf(mla_w_uv[l]).T)

        o_mla, (wbm, wbd, wo) = _mla_attn(
            qt.reshape((b, pieces) + qt.shape[1:]), k.reshape(b, s, -1),
            vt.reshape((b, pieces) + vt.shape[1:]),
            (w_branch_mla[l], w_branch_diff[l], w_out[l]))
        o_diff = _diff_attn(
            slopes, dqt.reshape(b, pieces, DIFF_HEADS, 2, DIFF_QK_DIM, PROJ_TM),
            dk.reshape(b, s, -1), dvt.reshape((b, pieces) + dvt.shape[1:]), positions,
            diff_lambda_q1[l][None, :], diff_lambda_k1[l][None, :],
            diff_lambda_q2[l][None, :], diff_lambda_k2[l][None, :],
            diff_subln_g[l][None, :], lambda_init=lambda_init)

        h, (w2g, w2u, w2d) = _merge_ln(
            h, o_mla.reshape(t, -1), o_diff.reshape(t, -1), wgm, wgd, wbm, wbd, wo,
            ln2_g[l][None, :], ln2_b[l][None, :],
            (ffn2_w_gate[l], ffn2_w_up[l], ffn2_w_down[l]))

        h = _ffn_ln(h, w2g, w2u, w2d, ln3_g[l][None, :], ln3_b[l][None, :])
    return h.reshape(b, s, d)
```

```python
import functools
import math

import jax
import jax.numpy as jnp
from jax import lax
from jax.experimental import pallas as pl
from jax.experimental.pallas import tpu as pltpu

D_MODEL = 2048
DEPTH = 1
MLA_HEADS = 8
MLA_Q_RANK = 512
MLA_KV_RANK = 512
MLA_NOPE_DIM = 128
MLA_ROPE_DIM = 64
MLA_V_DIM = 128
MLA_QK_DIM = MLA_NOPE_DIM + MLA_ROPE_DIM
MLA_QK_PAD = 256
ROPE_THETA = 10000.0
DIFF_HEADS = 4
DIFF_QK_DIM = 128
DIFF_V_DIM = 2 * DIFF_QK_DIM
DIFF_QK_COLS = DIFF_HEADS * 2 * DIFF_QK_DIM
DIFF_V_COLS = DIFF_HEADS * DIFF_V_DIM
D_FF = 5632
LN_EPS = 1e-5
RMS_EPS = 1e-6
ALPHA = (2 * DEPTH) ** 0.25
LOG2E = math.log2(math.e)

LANE = 128
MXU_DIM = 256
ONES_ROWS = 16
VMEM_LIMIT_BYTES = 56 * 1024 * 1024
VMEM_INTERNAL_SCRATCH_BYTES = 8 * 1024 * 1024
PROJ_TM = MXU_DIM
ATTN_TQ = 2048
ATTN_TK = 512
ATTN_GW = 2 * MXU_DIM
GROUP_PIECES = ATTN_GW // PROJ_TM
LOOKAHEAD = 2
SCORE_RING = 4
MLA_CHUNKS_PER_ITER = 1

F32 = jnp.float32
BF16 = jnp.bfloat16


def _dot(a, b):
    return jnp.dot(a, b, preferred_element_type=F32)


def _dot_nt(a, b):
    return lax.dot_general(a, b, (((1,), (1,)), ((), ())), preferred_element_type=F32)


def _layer_norm_rows(y, g, b):
    mu = jnp.mean(y, axis=-1, keepdims=True)
    d = y - mu
    var = jnp.mean(d * d, axis=-1, keepdims=True)
    return d * lax.rsqrt(var + LN_EPS) * g + b


def _rms_norm_rows(y, g):
    ms = jnp.mean(y * y, axis=-1, keepdims=True)
    return y * lax.rsqrt(ms + RMS_EPS) * g


def _resident(shape):
    return pl.BlockSpec(shape, lambda *_: (0,) * len(shape), pipeline_mode=pl.Buffered(1))


def _cast_slab_spec(w, n_steps, step_of=lambda *idx: idx[0]):
    rows, cols = w.shape
    slabs = n_steps
    while rows % slabs or (rows // slabs) % 16:
        slabs //= 2
    per = n_steps // slabs
    return pl.BlockSpec((rows // slabs, cols), lambda *idx: (step_of(*idx) // per, 0))


def _cast_slabs(refs):
    n = len(refs) // 2
    for src, dst in zip(refs[:n], refs[n:]):
        dst[...] = src[...].astype(BF16)


def _ffn_step(x_ref, wg_ref, wu_ref, wd_ref, g_ref, b_ref, o_ref, xb_ref):
    j = pl.program_id(1)

    @pl.when(j == 0)
    def _():
        x = x_ref[...]
        xb_ref[...] = x.astype(BF16)
        o_ref[...] = ALPHA * x

    xb = xb_ref[...]
    gate = _dot(xb, wg_ref[...])
    up = _dot(xb, wu_ref[...])
    act = (0.5 * (gate * jax.nn.sigmoid(gate))) * up
    o_ref[...] += _dot(act.astype(BF16), wd_ref[...])

    @pl.when(j == pl.num_programs(1) - 1)
    def _():
        o_ref[...] = _layer_norm_rows(o_ref[...], g_ref[...], b_ref[...])


def _ffn_ln_kernel(x_ref, wg_ref, wu_ref, wd_ref, g_ref, b_ref, o_ref, xb_ref):
    _ffn_step(x_ref, wg_ref, wu_ref, wd_ref, g_ref, b_ref, o_ref, xb_ref)


def _ffn_ln_tail_kernel(x_ref, wg_ref, wu_ref, wd_ref, g_ref, b_ref, head_ref,
                        o_ref, xb_ref, sem):
    i = pl.program_id(0)

    @pl.when(jnp.logical_and(i == 0, pl.program_id(1) == 0))
    def _():
        copy = pltpu.make_async_copy(head_ref, o_ref, sem)
        copy.start()
        copy.wait()

    @pl.when(i > 0)
    def _():
        _ffn_step(x_ref, wg_ref, wu_ref, wd_ref, g_ref, b_ref, o_ref, xb_ref)


def _ffn_ln_head_kernel(x_ref, wg_ref, wu_ref, wd_ref, g_ref, b_ref,
                        o_ref, wgb_ref, wub_ref, wdb_ref, xb_ref):
    _cast_slabs((wg_ref, wu_ref, wd_ref, wgb_ref, wub_ref, wdb_ref))
    _ffn_step(x_ref, wgb_ref, wub_ref, wdb_ref, g_ref, b_ref, o_ref, xb_ref)


def _ffn_specs(tm, tf, d, skip_first_tile=False):
    col = (lambda i, j: jnp.where(i == 0, 0, j)) if skip_first_tile else (lambda i, j: j)
    return [
        pl.BlockSpec((tm, d), lambda i, j: (i, 0)),
        pl.BlockSpec((d, tf), lambda i, j: (0, col(i, j))),
        pl.BlockSpec((d, tf), lambda i, j: (0, col(i, j))),
        pl.BlockSpec((tf, d), lambda i, j: (col(i, j), 0)),
        pl.BlockSpec((1, d), lambda i, j: (0, 0)),
        pl.BlockSpec((1, d), lambda i, j: (0, 0)),
    ]


def _ffn_ln_head(x, wg, wu, wd, g, b, *, tm=1024, tf=256):
    d = x.shape[1]
    f = wg.shape[1]
    specs = _ffn_specs(tm, tf, d)
    vmem_bytes = ((2 * tm * d * 4) * 2 + tm * d * 2 + 2 * (3 * d * tf * (4 + 2))
                  + tm * tf * (4 + 4 + 2))
    return pl.pallas_call(
        _ffn_ln_head_kernel,
        grid=(1, f // tf),
        in_specs=specs,
        out_specs=[specs[0]] + specs[1:4],
        out_shape=[jax.ShapeDtypeStruct((tm, d), F32)]
        + [jax.ShapeDtypeStruct(w.shape, BF16) for w in (wg, wu, wd)],
        scratch_shapes=[pltpu.VMEM((tm, d), BF16)],
        compiler_params=pltpu.CompilerParams(
            dimension_semantics=("arbitrary", "arbitrary"),
            vmem_limit_bytes=vmem_bytes + VMEM_INTERNAL_SCRATCH_BYTES),
        name="ffn_ln_head",
    )(x, wg, wu, wd, g, b)


def _ffn_ln(x, wg, wu, wd, g, b, *, head_tile=None, tm=1024, tf=512):
    t, d = x.shape
    f = wg.shape[1]
    with_head = head_tile is not None
    specs = _ffn_specs(tm, tf, d, skip_first_tile=with_head)
    vmem_bytes = (2 * tm * d * 4) * 2 + tm * d * 2 + 2 * (3 * d * tf * 2) + tm * tf * (4 + 4 + 2)
    return pl.pallas_call(
        _ffn_ln_tail_kernel if with_head else _ffn_ln_kernel,
        grid=(t // tm, f // tf),
        in_specs=specs + ([pl.BlockSpec(memory_space=pl.ANY)] if with_head else []),
        out_specs=specs[0],
        out_shape=jax.ShapeDtypeStruct((t, d), F32),
        scratch_shapes=[pltpu.VMEM((tm, d), BF16)]
        + ([pltpu.SemaphoreType.DMA(())] if with_head else []),
        compiler_params=pltpu.CompilerParams(
            dimension_semantics=("arbitrary", "arbitrary"),
            vmem_limit_bytes=vmem_bytes + VMEM_INTERNAL_SCRATCH_BYTES),
        name="ffn_ln",
    )(x, wg, wu, wd, g, b, *((head_tile,) if with_head else ()))


def _mixer_proj_kernel(h_ref, pos_ref, invf_ref, wlat_ref, wkrt_ref, wdqt_ref, wdk_ref, wdvt_ref,
                       qg_ref, kvg_ref, wqt_ref, wuk_ref, wuvt_ref,
                       qt_ref, k_ref, vt_ref, dqt_ref, dk_ref, dvt_ref):
    hb = h_ref[...].astype(BF16)
    tm = hb.shape[0]

    dq_scale = DIFF_QK_DIM ** -0.5 * LOG2E
    dqt_ref[...] = (_dot_nt(wdqt_ref[...], hb) * dq_scale).astype(BF16)
    dk_ref[...] = _dot(hb, wdk_ref[...]).astype(BF16)
    ones_rows = (lax.broadcasted_iota(jnp.int32, (ONES_ROWS, tm), 0) == 0).astype(BF16)
    dvt = _dot_nt(wdvt_ref[...], hb).astype(BF16)
    for h in range(DIFF_HEADS):
        dvt_ref[h, 0:DIFF_V_DIM, :] = dvt[h * DIFF_V_DIM:(h + 1) * DIFF_V_DIM, :]
        dvt_ref[h, DIFF_V_DIM:, :] = ones_rows

    lat = _dot(hb, wlat_ref[...])
    c_q = lat[:, 0:MLA_Q_RANK]
    c_kv = lat[:, MLA_Q_RANK:MLA_Q_RANK + MLA_KV_RANK]
    k_r_t = _dot_nt(wkrt_ref[...], hb)

    ang = invf_ref[...] * pos_ref[...]
    cos_f, sin_f = jnp.cos(ang), jnp.sin(ang)
    half = MLA_ROPE_DIM // 2
    pad = jnp.zeros((LANE - MLA_ROPE_DIM, tm), F32)

    def rope_t(r):
        x1, x2 = r[0:half, :], r[half:2 * half, :]
        return jnp.concatenate([x1 * cos_f - x2 * sin_f, x1 * sin_f + x2 * cos_f, pad], axis=0)

    cqn = _rms_norm_rows(c_q, qg_ref[...]).astype(BF16)
    ckvn = _rms_norm_rows(c_kv, kvg_ref[...]).astype(BF16)

    q_t = _dot_nt(wqt_ref[...], cqn)
    k_nope = _dot(ckvn, wuk_ref[...])
    vt = _dot_nt(wuvt_ref[...], ckvn).astype(BF16)
    for h in range(MLA_HEADS):
        vt_ref[h, 0:MLA_V_DIM, :] = vt[h * MLA_V_DIM:(h + 1) * MLA_V_DIM, :]
        vt_ref[h, MLA_V_DIM:, :] = ones_rows

    q_scale = MLA_QK_DIM ** -0.5 * LOG2E
    k_rope = rope_t(k_r_t).T.astype(BF16)
    for h in range(MLA_HEADS):
        c0 = h * MLA_QK_PAD
        qt_ref[h, 0:LANE, :] = (q_t[c0:c0 + LANE, :] * q_scale).astype(BF16)
        q_rope = rope_t(q_t[c0 + LANE:c0 + LANE + MLA_ROPE_DIM, :])
        qt_ref[h, LANE:2 * LANE, :] = (q_rope * q_scale).astype(BF16)
        k_ref[:, c0:c0 + LANE] = k_nope[:, h * LANE:(h + 1) * LANE].astype(BF16)
        k_ref[:, c0 + LANE:c0 + 2 * LANE] = k_rope


def _mixer_proj(h, pos, invf, wlat, wkrt, wdqt, wdk, wdvt, qg, kvg, wqt, wuk, wuvt):
    t, d = h.shape
    tm = PROJ_TM
    qk_cols = MLA_HEADS * MLA_QK_PAD
    row = lambda n: pl.BlockSpec((tm, n), lambda i: (i, 0))
    tile_t = lambda *dims: pl.BlockSpec((None,) + dims + (tm,),
                                        lambda i: (i,) + (0,) * (len(dims) + 1))
    rows_out = lambda n: jax.ShapeDtypeStruct((t, n), BF16)
    tile_t_out = lambda *dims: jax.ShapeDtypeStruct((t // tm,) + dims + (tm,), BF16)
    weights = (invf, wlat, wkrt, wdqt, wdk, wdvt, qg, kvg, wqt, wuk, wuvt)
    return pl.pallas_call(
        _mixer_proj_kernel,
        grid=(t // tm,),
        in_specs=[row(d), tile_t(1)] + [_resident(w.shape) for w in weights],
        out_specs=[tile_t(MLA_HEADS, MLA_QK_PAD), row(qk_cols),
                   tile_t(MLA_HEADS, MLA_V_DIM + ONES_ROWS),
                   tile_t(DIFF_QK_COLS), row(DIFF_QK_COLS),
                   tile_t(DIFF_HEADS, DIFF_V_DIM + ONES_ROWS)],
        out_shape=[tile_t_out(MLA_HEADS, MLA_QK_PAD), rows_out(qk_cols),
                   tile_t_out(MLA_HEADS, MLA_V_DIM + ONES_ROWS),
                   tile_t_out(DIFF_QK_COLS), rows_out(DIFF_QK_COLS),
                   tile_t_out(DIFF_HEADS, DIFF_V_DIM + ONES_ROWS)],
        compiler_params=pltpu.CompilerParams(
            dimension_semantics=("parallel",), vmem_limit_bytes=VMEM_LIMIT_BYTES),
        name="mixer_proj",
    )(h, pos, *weights)


def _diag_units(tq, tk):
    units = []
    for j in range(tq // tk):
        for g in range(tq // ATTN_GW):
            if j * tk >= (g + 1) * ATTN_GW:
                continue
            visible = (j + 1) * tk <= g * ATTN_GW + 1
            units.append((j, g, None if visible else j * tk))
    return units


def _causal_mask_t(s, g, key_off):
    rows = lax.broadcasted_iota(jnp.int32, s.shape, 0)
    cols = lax.broadcasted_iota(jnp.int32, s.shape, 1)
    return jnp.where(rows + key_off <= cols + g * ATTN_GW, s, -jnp.inf)


def _run_pipelined(units, s_refs, score_fn, consume_fn, next_units=()):
    ring = len(s_refs)
    todo = list(enumerate(list(units) + list(next_units)))[LOOKAHEAD:]
    for k, u in enumerate(units):
        if todo:
            kk, nu = todo.pop(0)
            score_fn(nu, s_refs[kk % ring])
        consume_fn(u, s_refs[k % ring][...])


def _prime_pipeline(units, s_refs, score_fn):
    for k, u in enumerate(units[:LOOKAHEAD]):
        score_fn(u, s_refs[k])


def _softmax_step_t(s, vt_pieces, m_ref, acc_ref, cols, exp_dtype):
    m_prev = m_ref[:, cols]
    m_new = jnp.maximum(m_prev, jnp.max(s, axis=0, keepdims=True))
    corr = jnp.exp2(m_prev - m_new)
    p = jnp.exp2((s - m_new).astype(exp_dtype)).astype(BF16)
    pv = None
    for u, vt in enumerate(vt_pieces):
        part = _dot(vt, p[u * PROJ_TM:(u + 1) * PROJ_TM, :])
        pv = part if pv is None else pv + part
    acc_ref[:, cols] = corr * acc_ref[:, cols] + pv
    m_ref[:, cols] = m_new


def _init_stats(m_ref, acc_ref):
    m_ref[...] = jnp.full(m_ref.shape, -jnp.inf, F32)
    acc_ref[...] = jnp.zeros(acc_ref.shape, F32)


def _normalized(acc_ref, dv):
    return acc_ref[0:dv, :] * (1.0 / acc_ref[dv:dv + 1, :])


def _mla_attn_kernel(qt_ref, k_ref, vt_ref, *rest, tq, tk, n_cast):
    cast_in, o_ref, cast_out = rest[:n_cast], rest[n_cast], rest[n_cast + 1:2 * n_cast + 1]
    m_ref, acc_ref, *s_refs = rest[2 * n_cast + 1:]
    _cast_slabs(cast_in + cast_out)
    qi = pl.program_id(2)
    n_piece = tk // PROJ_TM
    n_group = tq // ATTN_GW
    n_full = qi * (tq // tk)
    _init_stats(m_ref, acc_ref)

    def score(unit, s_ref):
        c, g, _ = unit
        kc = k_ref[pl.ds(pl.multiple_of(c * tk, tk), tk), :]
        for half in range(GROUP_PIECES):
            s_ref[:, half * PROJ_TM:(half + 1) * PROJ_TM] = _dot(kc, qt_ref[g * GROUP_PIECES + half])

    def consume(unit, s):
        c, g, key_off = unit
        if key_off is not None:
            s = _causal_mask_t(s, g, key_off)
        vts = [vt_ref[c * n_piece + u] for u in range(n_piece)]
        _softmax_step_t(s, vts, m_ref, acc_ref, slice(g * ATTN_GW, (g + 1) * ATTN_GW), F32)

    step = max(MLA_CHUNKS_PER_ITER, len(s_refs) // n_group)
    assert (step * n_group) % len(s_refs) == 0 and (tq // tk) % step == 0

    def chunk_units(c0):
        return [(c0 + dc, g, None) for dc in range(step) for g in range(n_group)]

    def body(i, carry):
        c0 = i * step
        _run_pipelined(chunk_units(c0), s_refs, score, consume, next_units=chunk_units(c0 + step))
        return carry

    _prime_pipeline(chunk_units(0), s_refs, score)
    lax.fori_loop(0, n_full // step, body, 0)
    diag = [(n_full + j, g, off) for j, g, off in _diag_units(tq, tk)]
    _run_pipelined(diag, s_refs, score, consume)
    o_ref[...] = _normalized(acc_ref, MLA_V_DIM).T.astype(o_ref.dtype)


def _mla_attn(qt, k, vt, cast_weights):
    b, s, _ = k.shape
    tq, tk = ATTN_TQ, ATTN_TK
    rows = MLA_V_DIM + ONES_ROWS
    nq = s // tq
    steps = b * MLA_HEADS * nq
    cast_specs = [_cast_slab_spec(w, steps, lambda bi, h, i: (bi * MLA_HEADS + h) * nq + i)
                  for w in cast_weights]
    outs = pl.pallas_call(
        functools.partial(_mla_attn_kernel, tq=tq, tk=tk, n_cast=len(cast_weights)),
        grid=(b, MLA_HEADS, nq),
        in_specs=[
            pl.BlockSpec((None, tq // PROJ_TM, None, MLA_QK_PAD, PROJ_TM),
                         lambda bi, h, i: (bi, i, h, 0, 0)),
            pl.BlockSpec((None, s, MLA_QK_PAD), lambda bi, h, i: (bi, 0, h)),
            pl.BlockSpec((None, s // PROJ_TM, None, rows, PROJ_TM),
                         lambda bi, h, i: (bi, 0, h, 0, 0)),
        ] + cast_specs,
        out_specs=[pl.BlockSpec((None, tq, MLA_V_DIM), lambda bi, h, i: (bi, i, h))] + cast_specs,
        out_shape=[jax.ShapeDtypeStruct((b, s, MLA_HEADS * MLA_V_DIM), BF16)]
        + [jax.ShapeDtypeStruct(w.shape, BF16) for w in cast_weights],
        scratch_shapes=[pltpu.VMEM((1, tq), F32), pltpu.VMEM((rows, tq), F32)]
        + [pltpu.VMEM((tk, ATTN_GW), F32)] * SCORE_RING,
        compiler_params=pltpu.CompilerParams(
            dimension_semantics=("arbitrary", "arbitrary", "arbitrary"),
            vmem_limit_bytes=VMEM_LIMIT_BYTES),
        name="mla_attn",
    )(qt, k, vt, *cast_weights)
    return outs[0], outs[1:]


def _diff_attn_kernel(slopes_ref, qt_ref, k_ref, vt_ref, posq_ref, posk_ref,
                      lq1_ref, lk1_ref, lq2_ref, lk2_ref, sg_ref, o_ref,
                      m0_ref, acc0_ref, m1_ref, acc1_ref, *s_refs, tq, tk, lambda_init):
    h = pl.program_id(1)
    qi = pl.program_id(2)
    n_piece = tk // PROJ_TM
    n_group = tq // ATTN_GW
    n_full = qi * (tq // tk)
    slope = slopes_ref[h] * LOG2E
    stats = ((m0_ref, acc0_ref), (m1_ref, acc1_ref))
    for m_ref, acc_ref in stats:
        _init_stats(m_ref, acc_ref)

    def make_fns():
        bias_cache = {}

        def score(unit, s_ref):
            ckey, c, g, mi, _ = unit
            rows = pl.ds(pl.multiple_of(c * tk, tk), tk)
            if (ckey, g) not in bias_cache:
                posk = posk_ref[rows, :]
                posq = posq_ref[:, g * ATTN_GW:(g + 1) * ATTN_GW]
                bias_cache[(ckey, g)] = slope * jnp.abs(posk - posq)
            bias = bias_cache[(ckey, g)]
            kc = k_ref[rows, mi * DIFF_QK_DIM:(mi + 1) * DIFF_QK_DIM]
            for half in range(GROUP_PIECES):
                lanes = slice(half * PROJ_TM, (half + 1) * PROJ_TM)
                s_ref[:, lanes] = _dot(kc, qt_ref[g * GROUP_PIECES + half, mi]) - bias[:, lanes]

        def consume(unit, s):
            _, c, g, mi, key_off = unit
            if key_off is not None:
                s = _causal_mask_t(s, g, key_off)
            m_ref, acc_ref = stats[mi]
            vts = [vt_ref[c * n_piece + u] for u in range(n_piece)]
            _softmax_step_t(s, vts, m_ref, acc_ref, slice(g * ATTN_GW, (g + 1) * ATTN_GW), BF16)

        return score, consume

    def chunk_units(ckey, c):
        return [(ckey, c, g, mi, None) for g in range(n_group) for mi in range(2)]

    assert (2 * n_group) % len(s_refs) == 0

    def body(c, carry):
        score, consume = make_fns()
        _run_pipelined(chunk_units("cur", c), s_refs, score, consume,
                       next_units=chunk_units("next", c + 1))
        return carry

    score, consume = make_fns()
    _prime_pipeline(chunk_units("first", 0), s_refs, score)
    lax.fori_loop(0, n_full, body, 0)
    diag = [(j, n_full + j, g, mi, off) for j, g, off in _diag_units(tq, tk) for mi in range(2)]
    _run_pipelined(diag, s_refs, score, consume)

    lam = (jnp.exp(jnp.sum(lq1_ref[...] * lk1_ref[...], axis=-1, keepdims=True))
           - jnp.exp(jnp.sum(lq2_ref[...] * lk2_ref[...], axis=-1, keepdims=True))
           + lambda_init)
    o_t = _normalized(acc0_ref, DIFF_V_DIM) - lam * _normalized(acc1_ref, DIFF_V_DIM)
    o = _rms_norm_rows(o_t.T, sg_ref[...]) * (1.0 - lambda_init)
    o_ref[...] = o.astype(o_ref.dtype)


def _diff_attn(slopes, dqt, dk, dvt, positions, lq1, lk1, lq2, lk2, sg, *, lambda_init):
    b, s, _ = dk.shape
    tq, tk = ATTN_TQ, ATTN_TK
    hd = 2 * DIFF_QK_DIM
    rows = DIFF_V_DIM + ONES_ROWS
    posf = positions.astype(F32)
    posq = posf.reshape(b, 1, s)
    posk = posf.reshape(b, s, 1)
    vec = lambda n: pl.BlockSpec((1, n), lambda bi, h, i: (0, 0))
    return pl.pallas_call(
        functools.partial(_diff_attn_kernel, tq=tq, tk=tk, lambda_init=lambda_init),
        grid=(b, DIFF_HEADS, s // tq),
        in_specs=[
            pl.BlockSpec(memory_space=pltpu.SMEM),
            pl.BlockSpec((None, tq // PROJ_TM, None, 2, DIFF_QK_DIM, PROJ_TM),
                         lambda bi, h, i: (bi, i, h, 0, 0, 0)),
            pl.BlockSpec((None, s, hd), lambda bi, h, i: (bi, 0, h)),
            pl.BlockSpec((None, s // PROJ_TM, None, rows, PROJ_TM),
                         lambda bi, h, i: (bi, 0, h, 0, 0)),
            pl.BlockSpec((None, 1, tq), lambda bi, h, i: (bi, 0, i)),
            pl.BlockSpec((None, s, 1), lambda bi, h, i: (bi, 0, 0)),
            vec(DIFF_QK_DIM), vec(DIFF_QK_DIM), vec(DIFF_QK_DIM), vec(DIFF_QK_DIM),
            vec(DIFF_V_DIM),
        ],
        out_specs=pl.BlockSpec((None, tq, DIFF_V_DIM), lambda bi, h, i: (bi, i, h)),
        out_shape=jax.ShapeDtypeStruct((b, s, DIFF_V_COLS), BF16),
        scratch_shapes=[pltpu.VMEM((1, tq), F32), pltpu.VMEM((rows, tq), F32),
                        pltpu.VMEM((1, tq), F32), pltpu.VMEM((rows, tq), F32)]
        + [pltpu.VMEM((tk, ATTN_GW), F32)] * SCORE_RING,
        compiler_params=pltpu.CompilerParams(
            dimension_semantics=("parallel", "parallel", "arbitrary"),
            vmem_limit_bytes=VMEM_LIMIT_BYTES),
        name="diff_attn",
    )(slopes, dqt, dk, dvt, posq, posk, lq1, lk1, lq2, lk2, sg)


def _merge_ln_kernel(h_ref, om_ref, od_ref, wgm_ref, wgd_ref, wbm_ref, wbd_ref, wo_ref,
                     g_ref, b_ref, *rest, tn):
    n_cast = (len(rest) - 2) // 2
    o_ref, y_ref = rest[n_cast], rest[-1]
    _cast_slabs(rest[:n_cast] + rest[n_cast + 1:-1])
    h = h_ref[...]
    hb = h.astype(BF16)
    om = om_ref[...]
    od = od_ref[...]
    for c in range(D_MODEL // tn):
        cols = slice(c * tn, (c + 1) * tn)
        gm = jax.nn.sigmoid(_dot(hb, wgm_ref[:, cols]))
        gd = jax.nn.sigmoid(_dot(hb, wgd_ref[:, cols]))
        y = gm * _dot(om, wbm_ref[:, cols]) + gd * _dot(od, wbd_ref[:, cols])
        y_ref[:, cols] = y.astype(BF16)
    mix = _dot(y_ref[...], wo_ref[...])
    o_ref[...] = _layer_norm_rows(ALPHA * h + mix, g_ref[...], b_ref[...])


def _merge_ln(h, om, od, wgm, wgd, wbm, wbd, wo, g, b, cast_weights, *, tm=256, tn=512):
    t, d = h.shape
    steps = t // tm
    row = lambda n: pl.BlockSpec((tm, n), lambda i: (i, 0))
    cast_specs = [_cast_slab_spec(w, steps) for w in cast_weights]
    outs = pl.pallas_call(
        functools.partial(_merge_ln_kernel, tn=tn),
        grid=(steps,),
        in_specs=[row(d), row(om.shape[1]), row(od.shape[1]),
                  _resident(wgm.shape), _resident(wgd.shape), _resident(wbm.shape),
                  _resident(wbd.shape), _resident(wo.shape),
                  _resident(g.shape), _resident(b.shape)] + cast_specs,
        out_specs=[row(d)] + cast_specs,
        out_shape=[jax.ShapeDtypeStruct((t, d), F32)]
        + [jax.ShapeDtypeStruct(w.shape, BF16) for w in cast_weights],
        scratch_shapes=[pltpu.VMEM((tm, d), BF16)],
        compiler_params=pltpu.CompilerParams(
            dimension_semantics=("arbitrary",), vmem_limit_bytes=VMEM_LIMIT_BYTES),
        name="merge_ln",
    )(h, om, od, wgm, wgd, wbm, wbd, wo, g, b, *cast_weights)
    return outs[0], outs[1:]


def _prep_mixer_weights(w_in, w_uq):
    base = MLA_Q_RANK + MLA_KV_RANK
    wlat = w_in[:, :base]
    wkrt = w_in[:, base:base + MLA_ROPE_DIM].T
    d0 = base + MLA_ROPE_DIM
    wdqt = w_in[:, d0:d0 + DIFF_QK_COLS].T
    wdk = w_in[:, d0 + DIFF_QK_COLS:d0 + 2 * DIFF_QK_COLS]
    v0 = d0 + 2 * DIFF_QK_COLS
    wdvt = w_in[:, v0:v0 + DIFF_V_COLS].T
    g0 = v0 + DIFF_V_COLS
    wgm = w_in[:, g0:g0 + D_MODEL]
    wgd = w_in[:, g0 + D_MODEL:g0 + 2 * D_MODEL]

    r = w_uq.shape[0]
    wq3 = w_uq.reshape(r, MLA_HEADS, MLA_QK_DIM)
    nope, rope = wq3[..., :MLA_NOPE_DIM], wq3[..., MLA_NOPE_DIM:]
    z3 = jnp.zeros((r, MLA_HEADS, LANE - MLA_ROPE_DIM), w_uq.dtype)
    wq = jnp.concatenate([nope, rope, z3], axis=-1).reshape(r, MLA_HEADS * MLA_QK_PAD)
    return wlat, wkrt, wdqt, wdk, wdvt, wgm, wgd, wq.T


def kernel(x, positions, ln1_g, ln1_b, ffn1_w_gate, ffn1_w_up, ffn1_w_down, w_in, mla_q_norm_g, mla_w_uq, mla_kv_norm_g, mla_w_uk, mla_w_uv, diff_lambda_q1, diff_lambda_k1, diff_lambda_q2, diff_lambda_k2, diff_subln_g, w_branch_mla, w_branch_diff, w_out, ln2_g, ln2_b, ffn2_w_gate, ffn2_w_up, ffn2_w_down, ln3_g, ln3_b):
    b, s, d = x.shape
    t = b * s
    bf = lambda w: w.astype(BF16)

    half = MLA_ROPE_DIM // 2
    inv_freq = ROPE_THETA ** (-jnp.arange(half, dtype=F32) / half)
    invf = inv_freq[:, None]
    slopes = 2.0 ** (-8.0 * jnp.arange(1, DIFF_HEADS + 1, dtype=F32) / DIFF_HEADS)

    h = x.reshape(t, d)
    for l in range(DEPTH):
        lambda_init = 0.8 - 0.6 * math.exp(-0.3 * l)
        head, w1g, w1u, w1d = _ffn_ln_head(h, ffn1_w_gate[l], ffn1_w_up[l], ffn1_w_down[l],
                                           ln1_g[l][None, :], ln1_b[l][None, :])
        h = _ffn_ln(h, w1g, w1u, w1d, ln1_g[l][None, :], ln1_b[l][None, :], head_tile=head)

        wlat, wkrt, wdqt, wdk, wdvt, wgm, wgd, wqt = _prep_mixer_weights(
            bf(w_in[l]), bf(mla_w_uq[l]))
        pieces = s // PROJ_TM
        qt, k, vt, dqt, dk, dvt = _mixer_proj(
            h, positions.reshape(t // PROJ_TM, 1, PROJ_TM).astype(F32), invf,
            wlat, wkrt, wdqt, wdk, wdvt, mla_q_norm_g[l][None, :], mla_kv_norm_g[l][None, :],
            wqt, bf(mla_w_uk[l]), bf(mla_w_uv[l]).T)

        o_mla, (wbm, wbd, wo) = _mla_attn(
            qt.reshape((b, pieces) + qt.shape[1:]), k.reshape(b, s, -1),
            vt.reshape((b, pieces) + vt.shape[1:]),
            (w_branch_mla[l], w_branch_diff[l], w_out[l]))
        o_diff = _diff_attn(
            slopes, dqt.reshape(b, pieces, DIFF_HEADS, 2, DIFF_QK_DIM, PROJ_TM),
            dk.reshape(b, s, -1), dvt.reshape((b, pieces) + dvt.shape[1:]), positions,
            diff_lambda_q1[l][None, :], diff_lambda_k1[l][None, :],
            diff_lambda_q2[l][None, :], diff_lambda_k2[l][None, :],
            diff_subln_g[l][None, :], lambda_init=lambda_init)

        h, (w2g, w2u, w2d) = _merge_ln(
            h, o_mla.reshape(t, -1), o_diff.reshape(t, -1), wgm, wgd, wbm, wbd, wo,
            ln2_g[l][None, :], ln2_b[l][None, :],
            (ffn2_w_gate[l], ffn2_w_up[l], ffn2_w_down[l]))

        h = _ffn_ln(h, w2g, w2u, w2d, ln3_g[l][None, :], ln3_b[l][None, :])
    return h.reshape(b, s, d)
```

```python
import functools
import math

import jax
import jax.numpy as jnp
from jax import lax
from jax.experimental import pallas as pl
from jax.experimental.pallas import tpu as pltpu

D_MODEL = 2048
DEPTH = 1
MLA_HEADS = 8
MLA_Q_RANK = 512
MLA_KV_RANK = 512
MLA_NOPE_DIM = 128
MLA_ROPE_DIM = 64
MLA_V_DIM = 128
MLA_QK_DIM = MLA_NOPE_DIM + MLA_ROPE_DIM
MLA_QK_PAD = 256
ROPE_THETA = 10000.0
DIFF_HEADS = 4
DIFF_QK_DIM = 128
DIFF_V_DIM = 2 * DIFF_QK_DIM
DIFF_QK_COLS = DIFF_HEADS * 2 * DIFF_QK_DIM
DIFF_V_COLS = DIFF_HEADS * DIFF_V_DIM
D_FF = 5632
LN_EPS = 1e-5
RMS_EPS = 1e-6
ALPHA = (2 * DEPTH) ** 0.25
LOG2E = math.log2(math.e)

LANE = 128
MXU_DIM = 256
ONES_ROWS = 16
VMEM_LIMIT_BYTES = 56 * 1024 * 1024
VMEM_INTERNAL_SCRATCH_BYTES = 8 * 1024 * 1024
PROJ_TM = MXU_DIM
ATTN_TQ = 2048
ATTN_TK = 512
ATTN_GW = 2 * MXU_DIM
GROUP_PIECES = ATTN_GW // PROJ_TM
LOOKAHEAD = 2
SCORE_RING = 4
MLA_CHUNKS_PER_ITER = 1

F32 = jnp.float32
BF16 = jnp.bfloat16


def _dot(a, b):
    return jnp.dot(a, b, preferred_element_type=F32)


def _dot_nt(a, b):
    return lax.dot_general(a, b, (((1,), (1,)), ((), ())), preferred_element_type=F32)


def _layer_norm_rows(y, g, b):
    mu = jnp.mean(y, axis=-1, keepdims=True)
    d = y - mu
    var = jnp.mean(d * d, axis=-1, keepdims=True)
    return d * lax.rsqrt(var + LN_EPS) * g + b


def _rms_norm_rows(y, g):
    ms = jnp.mean(y * y, axis=-1, keepdims=True)
    return y * lax.rsqrt(ms + RMS_EPS) * g


def _resident(shape):
    return pl.BlockSpec(shape, lambda *_: (0,) * len(shape), pipeline_mode=pl.Buffered(1))


def _cast_slab_spec(w, n_steps, step_of=lambda *idx: idx[0]):
    rows, cols = w.shape
    slabs = n_steps
    while rows % slabs or (rows // slabs) % 16:
        slabs //= 2
    per = n_steps // slabs
    return pl.BlockSpec((rows // slabs, cols), lambda *idx: (step_of(*idx) // per, 0))


def _cast_slabs(refs):
    n = len(refs) // 2
    for src, dst in zip(refs[:n], refs[n:]):
        dst[...] = src[...].astype(BF16)


def _ffn_step(x_ref, wg_ref, wu_ref, wd_ref, g_ref, b_ref, o_ref, xb_ref):
    j = pl.program_id(1)

    @pl.when(j == 0)
    def _():
        x = x_ref[...]
        xb_ref[...] = x.astype(BF16)
        o_ref[...] = ALPHA * x

    xb = xb_ref[...]
    gate = _dot(xb, wg_ref[...])
    up = _dot(xb, wu_ref[...])
    act = (0.5 * (gate * jax.nn.sigmoid(gate))) * up
    o_ref[...] += _dot(act.astype(BF16), wd_ref[...])

    @pl.when(j == pl.num_programs(1) - 1)
    def _():
        o_ref[...] = _layer_norm_rows(o_ref[...], g_ref[...], b_ref[...])


def _ffn_ln_kernel(x_ref, wg_ref, wu_ref, wd_ref, g_ref, b_ref, o_ref, xb_ref):
    _ffn_step(x_ref, wg_ref, wu_ref, wd_ref, g_ref, b_ref, o_ref, xb_ref)


def _ffn_ln_tail_kernel(x_ref, wg_ref, wu_ref, wd_ref, g_ref, b_ref, head_ref,
                        o_ref, xb_ref, sem):
    i = pl.program_id(0)

    @pl.when(jnp.logical_and(i == 0, pl.program_id(1) == 0))
    def _():
        copy = pltpu.make_async_copy(head_ref, o_ref, sem)
        copy.start()
        copy.wait()

    @pl.when(i > 0)
    def _():
        _ffn_step(x_ref, wg_ref, wu_ref, wd_ref, g_ref, b_ref, o_ref, xb_ref)


def _ffn_ln_head_kernel(x_ref, wg_ref, wu_ref, wd_ref, g_ref, b_ref,
                        o_ref, wgb_ref, wub_ref, wdb_ref, xb_ref):
    _cast_slabs((wg_ref, wu_ref, wd_ref, wgb_ref, wub_ref, wdb_ref))
    _ffn_step(x_ref, wgb_ref, wub_ref, wdb_ref, g_ref, b_ref, o_ref, xb_ref)


def _ffn_specs(tm, tf, d, skip_first_tile=False):
    col = (lambda i, j: jnp.where(i == 0, 0, j)) if skip_first_tile else (lambda i, j: j)
    return [
        pl.BlockSpec((tm, d), lambda i, j: (i, 0)),
        pl.BlockSpec((d, tf), lambda i, j: (0, col(i, j))),
        pl.BlockSpec((d, tf), lambda i, j: (0, col(i, j))),
        pl.BlockSpec((tf, d), lambda i, j: (col(i, j), 0)),
        pl.BlockSpec((1, d), lambda i, j: (0, 0)),
        pl.BlockSpec((1, d), lambda i, j: (0, 0)),
    ]


def _ffn_ln_head(x, wg, wu, wd, g, b, *, tm=1024, tf=256):
    d = x.shape[1]
    f = wg.shape[1]
    specs = _ffn_specs(tm, tf, d)
    vmem_bytes = ((2 * tm * d * 4) * 2 + tm * d * 2 + 2 * (3 * d * tf * (4 + 2))
                  + tm * tf * (4 + 4 + 2))
    return pl.pallas_call(
        _ffn_ln_head_kernel,
        grid=(1, f // tf),
        in_specs=specs,
        out_specs=[specs[0]] + specs[1:4],
        out_shape=[jax.ShapeDtypeStruct((tm, d), F32)]
        + [jax.ShapeDtypeStruct(w.shape, BF16) for w in (wg, wu, wd)],
        scratch_shapes=[pltpu.VMEM((tm, d), BF16)],
        compiler_params=pltpu.CompilerParams(
            dimension_semantics=("arbitrary", "arbitrary"),
            vmem_limit_bytes=vmem_bytes + VMEM_INTERNAL_SCRATCH_BYTES),
        name="ffn_ln_head",
    )(x, wg, wu, wd, g, b)


def _ffn_ln(x, wg, wu, wd, g, b, *, head_tile=None, tm=1024, tf=512):
    t, d = x.shape
    f = wg.shape[1]
    with_head = head_tile is not None
    specs = _ffn_specs(tm, tf, d, skip_first_tile=with_head)
    vmem_bytes = (2 * tm * d * 4) * 2 + tm * d * 2 + 2 * (3 * d * tf * 2) + tm * tf * (4 + 4 + 2)
    return pl.pallas_call(
        _ffn_ln_tail_kernel if with_head else _ffn_ln_kernel,
        grid=(t // tm, f // tf),
        in_specs=specs + ([pl.BlockSpec(memory_space=pl.ANY)] if with_head else []),
        out_specs=specs[0],
        out_shape=jax.ShapeDtypeStruct((t, d), F32),
        scratch_shapes=[pltpu.VMEM((tm, d), BF16)]
        + ([pltpu.SemaphoreType.DMA(())] if with_head else []),
        compiler_params=pltpu.CompilerParams(
            dimension_semantics=("arbitrary", "arbitrary"),
            vmem_limit_bytes=vmem_bytes + VMEM_INTERNAL_SCRATCH_BYTES),
        name="ffn_ln",
    )(x, wg, wu, wd, g, b, *((head_tile,) if with_head else ()))


def _mixer_proj_kernel(h_ref, pos_ref, invf_ref, wlat_ref, wkrt_ref, wdqt_ref, wdk_ref, wdvt_ref,
                       qg_ref, kvg_ref, wqt_ref, wuk_ref, wuvt_ref,
                       qt_ref, k_ref, vt_ref, dqt_ref, dk_ref, dvt_ref):
    hb = h_ref[...].astype(BF16)
    tm = hb.shape[0]

    dq_scale = DIFF_QK_DIM ** -0.5 * LOG2E
    dqt_ref[...] = (_dot_nt(wdqt_ref[...], hb) * dq_scale).astype(BF16)
    dk_ref[...] = _dot(hb, wdk_ref[...]).astype(BF16)
    ones_rows = (lax.broadcasted_iota(jnp.int32, (ONES_ROWS, tm), 0) == 0).astype(BF16)
    dvt = _dot_nt(wdvt_ref[...], hb).astype(BF16)
    for h in range(DIFF_HEADS):
        dvt_ref[h, 0:DIFF_V_DIM, :] = dvt[h * DIFF_V_DIM:(h + 1) * DIFF_V_DIM, :]
        dvt_ref[h, DIFF_V_DIM:, :] = ones_rows

    lat = _dot(hb, wlat_ref[...])
    c_q = lat[:, 0:MLA_Q_RANK]
    c_kv = lat[:, MLA_Q_RANK:MLA_Q_RANK + MLA_KV_RANK]
    k_r_t = _dot_nt(wkrt_ref[...], hb)

    ang = invf_ref[...] * pos_ref[...]
    cos_f, sin_f = jnp.cos(ang), jnp.sin(ang)
    half = MLA_ROPE_DIM // 2
    pad = jnp.zeros((LANE - MLA_ROPE_DIM, tm), F32)

    def rope_t(r):
        x1, x2 = r[0:half, :], r[half:2 * half, :]
        return jnp.concatenate([x1 * cos_f - x2 * sin_f, x1 * sin_f + x2 * cos_f, pad], axis=0)

    cqn = _rms_norm_rows(c_q, qg_ref[...]).astype(BF16)
    ckvn = _rms_norm_rows(c_kv, kvg_ref[...]).astype(BF16)

    q_t = _dot_nt(wqt_ref[...], cqn)
    k_nope = _dot(ckvn, wuk_ref[...])
    vt = _dot_nt(wuvt_ref[...], ckvn).astype(BF16)
    for h in range(MLA_HEADS):
        vt_ref[h, 0:MLA_V_DIM, :] = vt[h * MLA_V_DIM:(h + 1) * MLA_V_DIM, :]
        vt_ref[h, MLA_V_DIM:, :] = ones_rows

    q_scale = MLA_QK_DIM ** -0.5 * LOG2E
    k_rope = rope_t(k_r_t).T.astype(BF16)
    for h in range(MLA_HEADS):
        c0 = h * MLA_QK_PAD
        qt_ref[h, 0:LANE, :] = (q_t[c0:c0 + LANE, :] * q_scale).astype(BF16)
        q_rope = rope_t(q_t[c0 + LANE:c0 + LANE + MLA_ROPE_DIM, :])
        qt_ref[h, LANE:2 * LANE, :] = (q_rope * q_scale).astype(BF16)
        k_ref[:, c0:c0 + LANE] = k_nope[:, h * LANE:(h + 1) * LANE].astype(BF16)
        k_ref[:, c0 + LANE:c0 + 2 * LANE] = k_rope


def _mixer_proj(h, pos, invf, wlat, wkrt, wdqt, wdk, wdvt, qg, kvg, wqt, wuk, wuvt):
    t, d = h.shape
    tm = PROJ_TM
    qk_cols = MLA_HEADS * MLA_QK_PAD
    row = lambda n: pl.BlockSpec((tm, n), lambda i: (i, 0))
    tile_t = lambda *dims: pl.BlockSpec((None,) + dims + (tm,),
                                        lambda i: (i,) + (0,) * (len(dims) + 1))
    rows_out = lambda n: jax.ShapeDtypeStruct((t, n), BF16)
    tile_t_out = lambda *dims: jax.ShapeDtypeStruct((t // tm,) + dims + (tm,), BF16)
    weights = (invf, wlat, wkrt, wdqt, wdk, wdvt, qg, kvg, wqt, wuk, wuvt)
    return pl.pallas_call(
        _mixer_proj_kernel,
        grid=(t // tm,),
        in_specs=[row(d), tile_t(1)] + [_resident(w.shape) for w in weights],
        out_specs=[tile_t(MLA_HEADS, MLA_QK_PAD), row(qk_cols),
                   tile_t(MLA_HEADS, MLA_V_DIM + ONES_ROWS),
                   tile_t(DIFF_QK_COLS), row(DIFF_QK_COLS),
                   tile_t(DIFF_HEADS, DIFF_V_DIM + ONES_ROWS)],
        out_shape=[tile_t_out(MLA_HEADS, MLA_QK_PAD), rows_out(qk_cols),
                   tile_t_out(MLA_HEADS, MLA_V_DIM + ONES_ROWS),
                   tile_t_out(DIFF_QK_COLS), rows_out(DIFF_QK_COLS),
                   tile_t_out(DIFF_HEADS, DIFF_V_DIM + ONES_ROWS)],
        compiler_params=pltpu.CompilerParams(
            dimension_semantics=("parallel",), vmem_limit_bytes=VMEM_LIMIT_BYTES),
        name="mixer_proj",
    )(h, pos, *weights)


def _diag_units(tq, tk):
    units = []
    for j in range(tq // tk):
        for g in range(tq // ATTN_GW):
            if j * tk >= (g + 1) * ATTN_GW:
                continue
            visible = (j + 1) * tk <= g * ATTN_GW + 1
            units.append((j, g, None if visible else j * tk))
    return units


def _causal_mask_t(s, g, key_off):
    rows = lax.broadcasted_iota(jnp.int32, s.shape, 0)
    cols = lax.broadcasted_iota(jnp.int32, s.shape, 1)
    return jnp.where(rows + key_off <= cols + g * ATTN_GW, s, -jnp.inf)


def _run_pipelined(units, s_refs, score_fn, consume_fn, next_units=()):
    ring = len(s_refs)
    todo = list(enumerate(list(units) + list(next_units)))[LOOKAHEAD:]
    for k, u in enumerate(units):
        if todo:
            kk, nu = todo.pop(0)
            score_fn(nu, s_refs[kk % ring])
        consume_fn(u, s_refs[k % ring][...])


def _prime_pipeline(units, s_refs, score_fn):
    for k, u in enumerate(units[:LOOKAHEAD]):
        score_fn(u, s_refs[k])


def _softmax_step_t(s, vt_pieces, m_ref, acc_ref, cols):
    m_prev = m_ref[:, cols]
    m_new = jnp.maximum(m_prev, jnp.max(s, axis=0, keepdims=True))
    corr = jnp.exp2(m_prev - m_new)
    p = jnp.exp2(s - m_new).astype(BF16)
    pv = None
    for u, vt in enumerate(vt_pieces):
        part = _dot(vt, p[u * PROJ_TM:(u + 1) * PROJ_TM, :])
        pv = part if pv is None else pv + part
    acc_ref[:, cols] = corr * acc_ref[:, cols] + pv
    m_ref[:, cols] = m_new


def _init_stats(m_ref, acc_ref):
    m_ref[...] = jnp.full(m_ref.shape, -jnp.inf, F32)
    acc_ref[...] = jnp.zeros(acc_ref.shape, F32)


def _normalized(acc_ref, dv):
    return acc_ref[0:dv, :] * (1.0 / acc_ref[dv:dv + 1, :])


def _mla_attn_kernel(qt_ref, k_ref, vt_ref, *rest, tq, tk, n_cast):
    cast_in, o_ref, cast_out = rest[:n_cast], rest[n_cast], rest[n_cast + 1:2 * n_cast + 1]
    m_ref, acc_ref, *s_refs = rest[2 * n_cast + 1:]
    _cast_slabs(cast_in + cast_out)
    qi = pl.program_id(2)
    n_piece = tk // PROJ_TM
    n_group = tq // ATTN_GW
    n_full = qi * (tq // tk)
    _init_stats(m_ref, acc_ref)

    def score(unit, s_ref):
        c, g, _ = unit
        kc = k_ref[pl.ds(pl.multiple_of(c * tk, tk), tk), :]
        for half in range(GROUP_PIECES):
            s_ref[:, half * PROJ_TM:(half + 1) * PROJ_TM] = _dot(kc, qt_ref[g * GROUP_PIECES + half])

    def consume(unit, s):
        c, g, key_off = unit
        if key_off is not None:
            s = _causal_mask_t(s, g, key_off)
        vts = [vt_ref[c * n_piece + u] for u in range(n_piece)]
        _softmax_step_t(s, vts, m_ref, acc_ref, slice(g * ATTN_GW, (g + 1) * ATTN_GW))

    step = max(MLA_CHUNKS_PER_ITER, len(s_refs) // n_group)
    assert (step * n_group) % len(s_refs) == 0 and (tq // tk) % step == 0

    def chunk_units(c0):
        return [(c0 + dc, g, None) for dc in range(step) for g in range(n_group)]

    def body(i, carry):
        c0 = i * step
        _run_pipelined(chunk_units(c0), s_refs, score, consume, next_units=chunk_units(c0 + step))
        return carry

    _prime_pipeline(chunk_units(0), s_refs, score)
    lax.fori_loop(0, n_full // step, body, 0)
    diag = [(n_full + j, g, off) for j, g, off in _diag_units(tq, tk)]
    _run_pipelined(diag, s_refs, score, consume)
    o_ref[...] = _normalized(acc_ref, MLA_V_DIM).T.astype(o_ref.dtype)


def _mla_attn(qt, k, vt, cast_weights):
    b, s, _ = k.shape
    tq, tk = ATTN_TQ, ATTN_TK
    rows = MLA_V_DIM + ONES_ROWS
    nq = s // tq
    steps = b * MLA_HEADS * nq
    cast_specs = [_cast_slab_spec(w, steps, lambda bi, h, i: (bi * MLA_HEADS + h) * nq + i)
                  for w in cast_weights]
    outs = pl.pallas_call(
        functools.partial(_mla_attn_kernel, tq=tq, tk=tk, n_cast=len(cast_weights)),
        grid=(b, MLA_HEADS, nq),
        in_specs=[
            pl.BlockSpec((None, tq // PROJ_TM, None, MLA_QK_PAD, PROJ_TM),
                         lambda bi, h, i: (bi, i, h, 0, 0)),
            pl.BlockSpec((None, s, MLA_QK_PAD), lambda bi, h, i: (bi, 0, h)),
            pl.BlockSpec((None, s // PROJ_TM, None, rows, PROJ_TM),
                         lambda bi, h, i: (bi, 0, h, 0, 0)),
        ] + cast_specs,
        out_specs=[pl.BlockSpec((None, tq, MLA_V_DIM), lambda bi, h, i: (bi, i, h))] + cast_specs,
        out_shape=[jax.ShapeDtypeStruct((b, s, MLA_HEADS * MLA_V_DIM), BF16)]
        + [jax.ShapeDtypeStruct(w.shape, BF16) for w in cast_weights],
        scratch_shapes=[pltpu.VMEM((1, tq), F32), pltpu.VMEM((rows, tq), F32)]
        + [pltpu.VMEM((tk, ATTN_GW), F32)] * SCORE_RING,
        compiler_params=pltpu.CompilerParams(
            dimension_semantics=("arbitrary", "arbitrary", "arbitrary"),
            vmem_limit_bytes=VMEM_LIMIT_BYTES),
        name="mla_attn",
    )(qt, k, vt, *cast_weights)
    return outs[0], outs[1:]


def _diff_attn_kernel(slopes_ref, qt_ref, k_ref, vt_ref, posq_ref, posk_ref,
                      lq1_ref, lk1_ref, lq2_ref, lk2_ref, sg_ref, o_ref,
                      m0_ref, acc0_ref, m1_ref, acc1_ref, *s_refs, tq, tk, lambda_init):
    h = pl.program_id(1)
    qi = pl.program_id(2)
    n_piece = tk // PROJ_TM
    n_group = tq // ATTN_GW
    n_full = qi * (tq // tk)
    slope = slopes_ref[h] * LOG2E
    stats = ((m0_ref, acc0_ref), (m1_ref, acc1_ref))
    for m_ref, acc_ref in stats:
        _init_stats(m_ref, acc_ref)

    def make_fns():
        bias_cache = {}

        def score(unit, s_ref):
            ckey, c, g, mi, _ = unit
            rows = pl.ds(pl.multiple_of(c * tk, tk), tk)
            if (ckey, g) not in bias_cache:
                posk = posk_ref[rows, :]
                posq = posq_ref[:, g * ATTN_GW:(g + 1) * ATTN_GW]
                bias_cache[(ckey, g)] = slope * jnp.abs(posk - posq)
            bias = bias_cache[(ckey, g)]
            kc = k_ref[rows, mi * DIFF_QK_DIM:(mi + 1) * DIFF_QK_DIM]
            for half in range(GROUP_PIECES):
                lanes = slice(half * PROJ_TM, (half + 1) * PROJ_TM)
                s_ref[:, lanes] = _dot(kc, qt_ref[g * GROUP_PIECES + half, mi]) - bias[:, lanes]

        def consume(unit, s):
            _, c, g, mi, key_off = unit
            if key_off is not None:
                s = _causal_mask_t(s, g, key_off)
            m_ref, acc_ref = stats[mi]
            vts = [vt_ref[c * n_piece + u] for u in range(n_piece)]
            _softmax_step_t(s, vts, m_ref, acc_ref, slice(g * ATTN_GW, (g + 1) * ATTN_GW))

        return score, consume

    def chunk_units(ckey, c):
        return [(ckey, c, g, mi, None) for g in range(n_group) for mi in range(2)]

    assert (2 * n_group) % len(s_refs) == 0

    def body(c, carry):
        score, consume = make_fns()
        _run_pipelined(chunk_units("cur", c), s_refs, score, consume,
                       next_units=chunk_units("next", c + 1))
        return carry

    score, consume = make_fns()
    _prime_pipeline(chunk_units("first", 0), s_refs, score)
    lax.fori_loop(0, n_full, body, 0)
    diag = [(j, n_full + j, g, mi, off) for j, g, off in _diag_units(tq, tk) for mi in range(2)]
    _run_pipelined(diag, s_refs, score, consume)

    lam = (jnp.exp(jnp.sum(lq1_ref[...] * lk1_ref[...], axis=-1, keepdims=True))
           - jnp.exp(jnp.sum(lq2_ref[...] * lk2_ref[...], axis=-1, keepdims=True))
           + lambda_init)
    o_t = _normalized(acc0_ref, DIFF_V_DIM) - lam * _normalized(acc1_ref, DIFF_V_DIM)
    o = _rms_norm_rows(o_t.T, sg_ref[...]) * (1.0 - lambda_init)
    o_ref[...] = o.astype(o_ref.dtype)


def _diff_attn(slopes, dqt, dk, dvt, positions, lq1, lk1, lq2, lk2, sg, *, lambda_init):
    b, s, _ = dk.shape
    tq, tk = ATTN_TQ, ATTN_TK
    hd = 2 * DIFF_QK_DIM
    rows = DIFF_V_DIM + ONES_ROWS
    posf = positions.astype(F32)
    posq = posf.reshape(b, 1, s)
    posk = posf.reshape(b, s, 1)
    vec = lambda n: pl.BlockSpec((1, n), lambda bi, h, i: (0, 0))
    return pl.pallas_call(
        functools.partial(_diff_attn_kernel, tq=tq, tk=tk, lambda_init=lambda_init),
        grid=(b, DIFF_HEADS, s // tq),
        in_specs=[
            pl.BlockSpec(memory_space=pltpu.SMEM),
            pl.BlockSpec((None, tq // PROJ_TM, None, 2, DIFF_QK_DIM, PROJ_TM),
                         lambda bi, h, i: (bi, i, h, 0, 0, 0)),
            pl.BlockSpec((None, s, hd), lambda bi, h, i: (bi, 0, h)),
            pl.BlockSpec((None, s // PROJ_TM, None, rows, PROJ_TM),
                         lambda bi, h, i: (bi, 0, h, 0, 0)),
            pl.BlockSpec((None, 1, tq), lambda bi, h, i: (bi, 0, i)),
            pl.BlockSpec((None, s, 1), lambda bi, h, i: (bi, 0, 0)),
            vec(DIFF_QK_DIM), vec(DIFF_QK_DIM), vec(DIFF_QK_DIM), vec(DIFF_QK_DIM),
            vec(DIFF_V_DIM),
        ],
        out_specs=pl.BlockSpec((None, tq, DIFF_V_DIM), lambda bi, h, i: (bi, i, h)),
        out_shape=jax.ShapeDtypeStruct((b, s, DIFF_V_COLS), BF16),
        scratch_shapes=[pltpu.VMEM((1, tq), F32), pltpu.VMEM((rows, tq), F32),
                        pltpu.VMEM((1, tq), F32), pltpu.VMEM((rows, tq), F32)]
        + [pltpu.VMEM((tk, ATTN_GW), F32)] * SCORE_RING,
        compiler_params=pltpu.CompilerParams(
            dimension_semantics=("parallel", "parallel", "arbitrary"),
            vmem_limit_bytes=VMEM_LIMIT_BYTES),
        name="diff_attn",
    )(slopes, dqt, dk, dvt, posq, posk, lq1, lk1, lq2, lk2, sg)


def _merge_ln_kernel(h_ref, om_ref, od_ref, wgm_ref, wgd_ref, wbm_ref, wbd_ref, wo_ref,
                     g_ref, b_ref, *rest, tn):
    n_cast = (len(rest) - 2) // 2
    o_ref, y_ref = rest[n_cast], rest[-1]
    _cast_slabs(rest[:n_cast] + rest[n_cast + 1:-1])
    h = h_ref[...]
    hb = h.astype(BF16)
    om = om_ref[...]
    od = od_ref[...]
    for c in range(D_MODEL // tn):
        cols = slice(c * tn, (c + 1) * tn)
        gm = jax.nn.sigmoid(_dot(hb, wgm_ref[:, cols]))
        gd = jax.nn.sigmoid(_dot(hb, wgd_ref[:, cols]))
        y = gm * _dot(om, wbm_ref[:, cols]) + gd * _dot(od, wbd_ref[:, cols])
        y_ref[:, cols] = y.astype(BF16)
    mix = _dot(y_ref[...], wo_ref[...])
    o_ref[...] = _layer_norm_rows(ALPHA * h + mix, g_ref[...], b_ref[...])


def _merge_ln(h, om, od, wgm, wgd, wbm, wbd, wo, g, b, cast_weights, *, tm=256, tn=512):
    t, d = h.shape
    steps = t // tm
    row = lambda n: pl.BlockSpec((tm, n), lambda i: (i, 0))
    cast_specs = [_cast_slab_spec(w, steps) for w in cast_weights]
    outs = pl.pallas_call(
        functools.partial(_merge_ln_kernel, tn=tn),
        grid=(steps,),
        in_specs=[row(d), row(om.shape[1]), row(od.shape[1]),
                  _resident(wgm.shape), _resident(wgd.shape), _resident(wbm.shape),
                  _resident(wbd.shape), _resident(wo.shape),
                  _resident(g.shape), _resident(b.shape)] + cast_specs,
        out_specs=[row(d)] + cast_specs,
        out_shape=[jax.ShapeDtypeStruct((t, d), F32)]
        + [jax.ShapeDtypeStruct(w.shape, BF16) for w in cast_weights],
        scratch_shapes=[pltpu.VMEM((tm, d), BF16)],
        compiler_params=pltpu.CompilerParams(
            dimension_semantics=("arbitrary",), vmem_limit_bytes=VMEM_LIMIT_BYTES),
        name="merge_ln",
    )(h, om, od, wgm, wgd, wbm, wbd, wo, g, b, *cast_weights)
    return outs[0], outs[1:]


W_IN_PREP_STEPS = 8
W_IN_N_TRANSPOSED = 4


def _w_in_prep_kernel(*refs):
    n = len(refs) // 2
    for k, (src, dst) in enumerate(zip(refs[:n], refs[n:])):
        w = src[...]
        dst[...] = (w.T if k < W_IN_N_TRANSPOSED else w).astype(BF16)


def _prep_w_in(w_in_t):
    d = w_in_t.shape[1]
    steps = W_IN_PREP_STEPS
    base = MLA_Q_RANK + MLA_KV_RANK
    d0 = base + MLA_ROPE_DIM
    v0 = d0 + 2 * DIFF_QK_COLS
    g0 = v0 + DIFF_V_COLS
    pieces = [(0, base), (d0 + DIFF_QK_COLS, DIFF_QK_COLS), (g0, D_MODEL), (g0 + D_MODEL, D_MODEL),
              (d0, DIFF_QK_COLS), (v0, DIFF_V_COLS)]
    in_specs, out_specs, out_shapes = [], [], []
    for k, (r0, rows) in enumerate(pieces):
        per = rows // steps
        in_specs.append(pl.BlockSpec((pl.Element(per), pl.Element(d)),
                                     lambda i, r0=r0, per=per: (pl.multiple_of(r0 + i * per, 64), 0)))
        if k < W_IN_N_TRANSPOSED:
            out_specs.append(pl.BlockSpec((d, per), lambda i: (0, i)))
            out_shapes.append(jax.ShapeDtypeStruct((d, rows), BF16))
        else:
            out_specs.append(pl.BlockSpec((per, d), lambda i: (i, 0)))
            out_shapes.append(jax.ShapeDtypeStruct((rows, d), BF16))
    in_specs.append(pl.BlockSpec((MLA_ROPE_DIM, d), lambda i: (base // MLA_ROPE_DIM, 0)))
    out_specs.append(pl.BlockSpec((MLA_ROPE_DIM, d), lambda i: (0, 0)))
    out_shapes.append(jax.ShapeDtypeStruct((MLA_ROPE_DIM, d), BF16))
    return pl.pallas_call(
        _w_in_prep_kernel,
        grid=(steps,),
        in_specs=in_specs,
        out_specs=out_specs,
        out_shape=out_shapes,
        compiler_params=pltpu.CompilerParams(
            dimension_semantics=("arbitrary",), vmem_limit_bytes=VMEM_LIMIT_BYTES),
        name="w_in_prep",
    )(*([w_in_t] * len(in_specs)))


def _prep_w_uq_t(w_uq):
    r = w_uq.shape[0]
    wq3 = w_uq.reshape(r, MLA_HEADS, MLA_QK_DIM)
    z3 = jnp.zeros((r, MLA_HEADS, MLA_QK_PAD - MLA_QK_DIM), w_uq.dtype)
    return jnp.concatenate([wq3, z3], axis=-1).reshape(r, MLA_HEADS * MLA_QK_PAD).T


def kernel(x, positions, ln1_g, ln1_b, ffn1_w_gate, ffn1_w_up, ffn1_w_down, w_in, mla_q_norm_g, mla_w_uq, mla_kv_norm_g, mla_w_uk, mla_w_uv, diff_lambda_q1, diff_lambda_k1, diff_lambda_q2, diff_lambda_k2, diff_subln_g, w_branch_mla, w_branch_diff, w_out, ln2_g, ln2_b, ffn2_w_gate, ffn2_w_up, ffn2_w_down, ln3_g, ln3_b):
    b, s, d = x.shape
    t = b * s
    bf = lambda w: w.astype(BF16)

    half = MLA_ROPE_DIM // 2
    inv_freq = ROPE_THETA ** (-jnp.arange(half, dtype=F32) / half)
    invf = inv_freq[:, None]
    slopes = 2.0 ** (-8.0 * jnp.arange(1, DIFF_HEADS + 1, dtype=F32) / DIFF_HEADS)

    h = x.reshape(t, d)
    for l in range(DEPTH):
        lambda_init = 0.8 - 0.6 * math.exp(-0.3 * l)
        head, w1g, w1u, w1d = _ffn_ln_head(h, ffn1_w_gate[l], ffn1_w_up[l], ffn1_w_down[l],
                                           ln1_g[l][None, :], ln1_b[l][None, :])
        h = _ffn_ln(h, w1g, w1u, w1d, ln1_g[l][None, :], ln1_b[l][None, :], head_tile=head)

        wlat, wdk, wgm, wgd, wdqt, wdvt, wkrt = _prep_w_in(w_in[l].T)
        wqt = _prep_w_uq_t(bf(mla_w_uq[l]))
        pieces = s // PROJ_TM
        qt, k, vt, dqt, dk, dvt = _mixer_proj(
            h, positions.reshape(t // PROJ_TM, 1, PROJ_TM).astype(F32), invf,
            wlat, wkrt, wdqt, wdk, wdvt, mla_q_norm_g[l][None, :], mla_kv_norm_g[l][None, :],
            wqt, bf(mla_w_uk[l]), bf(mla_w_uv[l]).T)

        o_mla, (wbm, wbd, wo) = _mla_attn(
            qt.reshape((b, pieces) + qt.shape[1:]), k.reshape(b, s, -1),
            vt.reshape((b, pieces) + vt.shape[1:]),
            (w_branch_mla[l], w_branch_diff[l], w_out[l]))
        o_diff = _diff_attn(
            slopes, dqt.reshape(b, pieces, DIFF_HEADS, 2, DIFF_QK_DIM, PROJ_TM),
            dk.reshape(b, s, -1), dvt.reshape((b, pieces) + dvt.shape[1:]), positions,
            diff_lambda_q1[l][None, :], diff_lambda_k1[l][None, :],
            diff_lambda_q2[l][None, :], diff_lambda_k2[l][None, :],
            diff_subln_g[l][None, :], lambda_init=lambda_init)

        h, (w2g, w2u, w2d) = _merge_ln(
            h, o_mla.reshape(t, -1), o_diff.reshape(t, -1), wgm, wgd, wbm, wbd, wo,
            ln2_g[l][None, :], ln2_b[l][None, :],
            (ffn2_w_gate[l], ffn2_w_up[l], ffn2_w_down[l]))

        h = _ffn_ln(h, w2g, w2u, w2d, ln3_g[l][None, :], ln3_b[l][None, :])
    return h.reshape(b, s, d)
```

```python
import functools
import math

import jax
import jax.numpy as jnp
from jax import lax
from jax.experimental import pallas as pl
from jax.experimental.pallas import tpu as pltpu

D_MODEL = 2048
DEPTH = 1
MLA_HEADS = 8
MLA_Q_RANK = 512
MLA_KV_RANK = 512
MLA_NOPE_DIM = 128
MLA_ROPE_DIM = 64
MLA_V_DIM = 128
MLA_QK_DIM = MLA_NOPE_DIM + MLA_ROPE_DIM
MLA_QK_PAD = 256
ROPE_THETA = 10000.0
DIFF_HEADS = 4
DIFF_QK_DIM = 128
DIFF_V_DIM = 2 * DIFF_QK_DIM
DIFF_QK_COLS = DIFF_HEADS * 2 * DIFF_QK_DIM
DIFF_V_COLS = DIFF_HEADS * DIFF_V_DIM
D_FF = 5632
LN_EPS = 1e-5
RMS_EPS = 1e-6
ALPHA = (2 * DEPTH) ** 0.25
LOG2E = math.log2(math.e)

LANE = 128
MXU_DIM = 256
ONES_ROWS = 16
VMEM_LIMIT_BYTES = 56 * 1024 * 1024
VMEM_INTERNAL_SCRATCH_BYTES = 8 * 1024 * 1024
PROJ_TM = MXU_DIM
ATTN_TQ = 2048
ATTN_TK = 512
ATTN_GW = 2 * MXU_DIM
GROUP_PIECES = ATTN_GW // PROJ_TM
LOOKAHEAD = 2
SCORE_RING = 4
HEADS_PER_STEP = 2

F32 = jnp.float32
BF16 = jnp.bfloat16


def _dot(a, b):
    return jnp.dot(a, b, preferred_element_type=F32)


def _dot_nt(a, b):
    return lax.dot_general(a, b, (((1,), (1,)), ((), ())), preferred_element_type=F32)


def _layer_norm_rows(y, g, b):
    mu = jnp.mean(y, axis=-1, keepdims=True)
    d = y - mu
    var = jnp.mean(d * d, axis=-1, keepdims=True)
    return d * lax.rsqrt(var + LN_EPS) * g + b


def _rms_norm_rows(y, g):
    ms = jnp.mean(y * y, axis=-1, keepdims=True)
    return y * lax.rsqrt(ms + RMS_EPS) * g


def _resident(shape):
    return pl.BlockSpec(shape, lambda *_: (0,) * len(shape), pipeline_mode=pl.Buffered(1))


def _cast_slab_spec(w, n_steps, step_of=lambda *idx: idx[0]):
    rows, cols = w.shape
    slabs = n_steps
    while rows % slabs or (rows // slabs) % 16:
        slabs //= 2
    per = n_steps // slabs
    return pl.BlockSpec((rows // slabs, cols), lambda *idx: (step_of(*idx) // per, 0))


def _cast_slabs(refs):
    n = len(refs) // 2
    for src, dst in zip(refs[:n], refs[n:]):
        dst[...] = src[...].astype(BF16)


def _ffn_step(x_ref, wg_ref, wu_ref, wd_ref, g_ref, b_ref, o_ref, xb_ref):
    j = pl.program_id(1)

    @pl.when(j == 0)
    def _():
        x = x_ref[...]
        xb_ref[...] = x.astype(BF16)
        o_ref[...] = ALPHA * x

    xb = xb_ref[...]
    gate = _dot(xb, wg_ref[...])
    up = _dot(xb, wu_ref[...])
    act = (0.5 * (gate * jax.nn.sigmoid(gate))) * up
    o_ref[...] += _dot(act.astype(BF16), wd_ref[...])

    @pl.when(j == pl.num_programs(1) - 1)
    def _():
        o_ref[...] = _layer_norm_rows(o_ref[...], g_ref[...], b_ref[...])


def _ffn_ln_kernel(x_ref, wg_ref, wu_ref, wd_ref, g_ref, b_ref, o_ref, xb_ref):
    _ffn_step(x_ref, wg_ref, wu_ref, wd_ref, g_ref, b_ref, o_ref, xb_ref)


def _ffn_ln_tail_kernel(x_ref, wg_ref, wu_ref, wd_ref, g_ref, b_ref, head_ref,
                        o_ref, xb_ref, sem):
    i = pl.program_id(0)

    @pl.when(jnp.logical_and(i == 0, pl.program_id(1) == 0))
    def _():
        copy = pltpu.make_async_copy(head_ref, o_ref, sem)
        copy.start()
        copy.wait()

    @pl.when(i > 0)
    def _():
        _ffn_step(x_ref, wg_ref, wu_ref, wd_ref, g_ref, b_ref, o_ref, xb_ref)


def _ffn_ln_head_kernel(x_ref, wg_ref, wu_ref, wd_ref, g_ref, b_ref,
                        o_ref, wgb_ref, wub_ref, wdb_ref, xb_ref):
    _cast_slabs((wg_ref, wu_ref, wd_ref, wgb_ref, wub_ref, wdb_ref))
    _ffn_step(x_ref, wgb_ref, wub_ref, wdb_ref, g_ref, b_ref, o_ref, xb_ref)


def _ffn_specs(tm, tf, d, skip_first_tile=False):
    col = (lambda i, j: jnp.where(i == 0, 0, j)) if skip_first_tile else (lambda i, j: j)
    return [
        pl.BlockSpec((tm, d), lambda i, j: (i, 0)),
        pl.BlockSpec((d, tf), lambda i, j: (0, col(i, j))),
        pl.BlockSpec((d, tf), lambda i, j: (0, col(i, j))),
        pl.BlockSpec((tf, d), lambda i, j: (col(i, j), 0)),
        pl.BlockSpec((1, d), lambda i, j: (0, 0)),
        pl.BlockSpec((1, d), lambda i, j: (0, 0)),
    ]


def _ffn_ln_head(x, wg, wu, wd, g, b, *, tm=1024, tf=256):
    d = x.shape[1]
    f = wg.shape[1]
    specs = _ffn_specs(tm, tf, d)
    vmem_bytes = ((2 * tm * d * 4) * 2 + tm * d * 2 + 2 * (3 * d * tf * (4 + 2))
                  + tm * tf * (4 + 4 + 2))
    return pl.pallas_call(
        _ffn_ln_head_kernel,
        grid=(1, f // tf),
        in_specs=specs,
        out_specs=[specs[0]] + specs[1:4],
        out_shape=[jax.ShapeDtypeStruct((tm, d), F32)]
        + [jax.ShapeDtypeStruct(w.shape, BF16) for w in (wg, wu, wd)],
        scratch_shapes=[pltpu.VMEM((tm, d), BF16)],
        compiler_params=pltpu.CompilerParams(
            dimension_semantics=("arbitrary", "arbitrary"),
            vmem_limit_bytes=vmem_bytes + VMEM_INTERNAL_SCRATCH_BYTES),
        name="ffn_ln_head",
    )(x, wg, wu, wd, g, b)


def _ffn_ln(x, wg, wu, wd, g, b, *, head_tile=None, tm=1024, tf=512):
    t, d = x.shape
    f = wg.shape[1]
    with_head = head_tile is not None
    specs = _ffn_specs(tm, tf, d, skip_first_tile=with_head)
    vmem_bytes = (2 * tm * d * 4) * 2 + tm * d * 2 + 2 * (3 * d * tf * 2) + tm * tf * (4 + 4 + 2)
    return pl.pallas_call(
        _ffn_ln_tail_kernel if with_head else _ffn_ln_kernel,
        grid=(t // tm, f // tf),
        in_specs=specs + ([pl.BlockSpec(memory_space=pl.ANY)] if with_head else []),
        out_specs=specs[0],
        out_shape=jax.ShapeDtypeStruct((t, d), F32),
        scratch_shapes=[pltpu.VMEM((tm, d), BF16)]
        + ([pltpu.SemaphoreType.DMA(())] if with_head else []),
        compiler_params=pltpu.CompilerParams(
            dimension_semantics=("arbitrary", "arbitrary"),
            vmem_limit_bytes=vmem_bytes + VMEM_INTERNAL_SCRATCH_BYTES),
        name="ffn_ln",
    )(x, wg, wu, wd, g, b, *((head_tile,) if with_head else ()))


def _mixer_proj_kernel(h_ref, pos_ref, invf_ref, wlat_ref, wkrt_ref, wdqt_ref, wdk_ref, wdvt_ref,
                       qg_ref, kvg_ref, wqt_ref, wuk_ref, wuvt_ref,
                       qt_ref, k_ref, vt_ref, dqt_ref, dk_ref, dvt_ref):
    hb = h_ref[...].astype(BF16)
    tm = hb.shape[0]

    dq_scale = DIFF_QK_DIM ** -0.5 * LOG2E
    dqt_ref[...] = (_dot_nt(wdqt_ref[...], hb) * dq_scale).astype(BF16)
    dk_ref[...] = _dot(hb, wdk_ref[...]).astype(BF16)
    ones_rows = (lax.broadcasted_iota(jnp.int32, (ONES_ROWS, tm), 0) == 0).astype(BF16)
    dvt = _dot_nt(wdvt_ref[...], hb).astype(BF16)
    for h in range(DIFF_HEADS):
        dvt_ref[h, 0:DIFF_V_DIM, :] = dvt[h * DIFF_V_DIM:(h + 1) * DIFF_V_DIM, :]
        dvt_ref[h, DIFF_V_DIM:, :] = ones_rows

    lat = _dot(hb, wlat_ref[...])
    c_q = lat[:, 0:MLA_Q_RANK]
    c_kv = lat[:, MLA_Q_RANK:MLA_Q_RANK + MLA_KV_RANK]
    k_r_t = _dot_nt(wkrt_ref[...], hb)

    ang = invf_ref[...] * pos_ref[...]
    cos_f, sin_f = jnp.cos(ang), jnp.sin(ang)
    half = MLA_ROPE_DIM // 2
    pad = jnp.zeros((LANE - MLA_ROPE_DIM, tm), F32)

    def rope_t(r):
        x1, x2 = r[0:half, :], r[half:2 * half, :]
        return jnp.concatenate([x1 * cos_f - x2 * sin_f, x1 * sin_f + x2 * cos_f, pad], axis=0)

    cqn = _rms_norm_rows(c_q, qg_ref[...]).astype(BF16)
    ckvn = _rms_norm_rows(c_kv, kvg_ref[...]).astype(BF16)

    q_t = _dot_nt(wqt_ref[...], cqn)
    k_nope = _dot(ckvn, wuk_ref[...])
    vt = _dot_nt(wuvt_ref[...], ckvn).astype(BF16)
    for h in range(MLA_HEADS):
        vt_ref[h, 0:MLA_V_DIM, :] = vt[h * MLA_V_DIM:(h + 1) * MLA_V_DIM, :]
        vt_ref[h, MLA_V_DIM:, :] = ones_rows

    q_scale = MLA_QK_DIM ** -0.5 * LOG2E
    k_rope = rope_t(k_r_t).T.astype(BF16)
    for h in range(MLA_HEADS):
        c0 = h * MLA_QK_PAD
        qt_ref[h, 0:LANE, :] = (q_t[c0:c0 + LANE, :] * q_scale).astype(BF16)
        q_rope = rope_t(q_t[c0 + LANE:c0 + LANE + MLA_ROPE_DIM, :])
        qt_ref[h, LANE:2 * LANE, :] = (q_rope * q_scale).astype(BF16)
        k_ref[:, c0:c0 + LANE] = k_nope[:, h * LANE:(h + 1) * LANE].astype(BF16)
        k_ref[:, c0 + LANE:c0 + 2 * LANE] = k_rope


def _mixer_proj(h, pos, invf, wlat, wkrt, wdqt, wdk, wdvt, qg, kvg, wqt, wuk, wuvt):
    t, d = h.shape
    tm = PROJ_TM
    qk_cols = MLA_HEADS * MLA_QK_PAD
    row = lambda n: pl.BlockSpec((tm, n), lambda i: (i, 0))
    tile_t = lambda *dims: pl.BlockSpec((None,) + dims + (tm,),
                                        lambda i: (i,) + (0,) * (len(dims) + 1))
    rows_out = lambda n: jax.ShapeDtypeStruct((t, n), BF16)
    tile_t_out = lambda *dims: jax.ShapeDtypeStruct((t // tm,) + dims + (tm,), BF16)
    weights = (invf, wlat, wkrt, wdqt, wdk, wdvt, qg, kvg, wqt, wuk, wuvt)
    return pl.pallas_call(
        _mixer_proj_kernel,
        grid=(t // tm,),
        in_specs=[row(d), tile_t(1)] + [_resident(w.shape) for w in weights],
        out_specs=[tile_t(MLA_HEADS, MLA_QK_PAD), row(qk_cols),
                   tile_t(MLA_HEADS, MLA_V_DIM + ONES_ROWS),
                   tile_t(DIFF_QK_COLS), row(DIFF_QK_COLS),
                   tile_t(DIFF_HEADS, DIFF_V_DIM + ONES_ROWS)],
        out_shape=[tile_t_out(MLA_HEADS, MLA_QK_PAD), rows_out(qk_cols),
                   tile_t_out(MLA_HEADS, MLA_V_DIM + ONES_ROWS),
                   tile_t_out(DIFF_QK_COLS), rows_out(DIFF_QK_COLS),
                   tile_t_out(DIFF_HEADS, DIFF_V_DIM + ONES_ROWS)],
        compiler_params=pltpu.CompilerParams(
            dimension_semantics=("parallel",), vmem_limit_bytes=VMEM_LIMIT_BYTES),
        name="mixer_proj",
    )(h, pos, *weights)


def _diag_units(tq, tk):
    units = []
    for j in range(tq // tk):
        for g in range(tq // ATTN_GW):
            if j * tk >= (g + 1) * ATTN_GW:
                continue
            visible = (j + 1) * tk <= g * ATTN_GW + 1
            units.append((j, g, None if visible else j * tk))
    return units


def _causal_mask_t(s, g, key_off):
    rows = lax.broadcasted_iota(jnp.int32, s.shape, 0)
    cols = lax.broadcasted_iota(jnp.int32, s.shape, 1)
    return jnp.where(rows + key_off <= cols + g * ATTN_GW, s, -jnp.inf)


def _run_pipelined(units, s_refs, score_fn, consume_fn, next_units=()):
    ring = len(s_refs)
    todo = list(enumerate(list(units) + list(next_units)))[LOOKAHEAD:]
    for k, u in enumerate(units):
        if todo:
            kk, nu = todo.pop(0)
            score_fn(nu, s_refs[kk % ring])
        consume_fn(u, s_refs[k % ring][...])


def _prime_pipeline(units, s_refs, score_fn):
    for k, u in enumerate(units[:LOOKAHEAD]):
        score_fn(u, s_refs[k])


def _softmax_step_t(s, vt_pieces, m_ref, acc_ref, cols):
    m_prev = m_ref[:, cols]
    m_new = jnp.maximum(m_prev, jnp.max(s, axis=0, keepdims=True))
    corr = jnp.exp2(m_prev - m_new)
    p = jnp.exp2(s - m_new).astype(BF16)
    pv = None
    for u, vt in enumerate(vt_pieces):
        part = _dot(vt, p[u * PROJ_TM:(u + 1) * PROJ_TM, :])
        pv = part if pv is None else pv + part
    acc_ref[:, cols] = corr * acc_ref[:, cols] + pv
    m_ref[:, cols] = m_new


def _init_stats(m_ref, acc_ref):
    m_ref[...] = jnp.full(m_ref.shape, -jnp.inf, F32)
    acc_ref[...] = jnp.zeros(acc_ref.shape, F32)


def _normalized(acc_ref, dv):
    return acc_ref[0:dv, :] * (1.0 / acc_ref[dv:dv + 1, :])


def _mla_attn_kernel(qt_ref, k_ref, vt_ref, *rest, tq, tk, n_cast):
    cast_in, o_ref, cast_out = rest[:n_cast], rest[n_cast], rest[n_cast + 1:2 * n_cast + 1]
    scratch = rest[2 * n_cast + 1:]
    stats = [scratch[2 * hh:2 * hh + 2] for hh in range(HEADS_PER_STEP)]
    s_refs = scratch[2 * HEADS_PER_STEP:]
    _cast_slabs(cast_in + cast_out)
    qi = pl.program_id(2)
    n_piece = tk // PROJ_TM
    n_group = tq // ATTN_GW
    n_full = qi * (tq // tk)
    for m_ref, acc_ref in stats:
        _init_stats(m_ref, acc_ref)

    def score(unit, s_ref):
        hh, c, g, _ = unit
        kc = k_ref[pl.ds(pl.multiple_of(c * tk, tk), tk), hh * MLA_QK_PAD:(hh + 1) * MLA_QK_PAD]
        for half in range(GROUP_PIECES):
            s_ref[:, half * PROJ_TM:(half + 1) * PROJ_TM] = _dot(
                kc, qt_ref[g * GROUP_PIECES + half, hh])

    def consume(unit, s):
        hh, c, g, key_off = unit
        if key_off is not None:
            s = _causal_mask_t(s, g, key_off)
        m_ref, acc_ref = stats[hh]
        vts = [vt_ref[c * n_piece + u, hh] for u in range(n_piece)]
        _softmax_step_t(s, vts, m_ref, acc_ref, slice(g * ATTN_GW, (g + 1) * ATTN_GW))

    def chunk_units(c):
        return [(hh, c, g, None) for g in range(n_group) for hh in range(HEADS_PER_STEP)]

    assert (n_group * HEADS_PER_STEP) % len(s_refs) == 0

    def body(c, carry):
        _run_pipelined(chunk_units(c), s_refs, score, consume, next_units=chunk_units(c + 1))
        return carry

    _prime_pipeline(chunk_units(0), s_refs, score)
    lax.fori_loop(0, n_full, body, 0)
    diag = [(hh, n_full + j, g, off) for j, g, off in _diag_units(tq, tk)
            for hh in range(HEADS_PER_STEP)]
    _run_pipelined(diag, s_refs, score, consume)
    for hh, (_, acc_ref) in enumerate(stats):
        o_ref[:, hh * MLA_V_DIM:(hh + 1) * MLA_V_DIM] = (
            _normalized(acc_ref, MLA_V_DIM).T.astype(o_ref.dtype))


def _mla_attn(qt, k, vt, cast_weights):
    b, s, _ = k.shape
    tq, tk = ATTN_TQ, ATTN_TK
    rows = MLA_V_DIM + ONES_ROWS
    nq = s // tq
    hp = HEADS_PER_STEP
    n_hp = MLA_HEADS // hp
    steps = b * n_hp * nq
    cast_specs = [_cast_slab_spec(w, steps, lambda bi, h, i: (bi * n_hp + h) * nq + i)
                  for w in cast_weights]
    outs = pl.pallas_call(
        functools.partial(_mla_attn_kernel, tq=tq, tk=tk, n_cast=len(cast_weights)),
        grid=(b, n_hp, nq),
        in_specs=[
            pl.BlockSpec((None, tq // PROJ_TM, hp, MLA_QK_PAD, PROJ_TM),
                         lambda bi, h, i: (bi, i, h, 0, 0)),
            pl.BlockSpec((None, s, hp * MLA_QK_PAD), lambda bi, h, i: (bi, 0, h)),
            pl.BlockSpec((None, s // PROJ_TM, hp, rows, PROJ_TM),
                         lambda bi, h, i: (bi, 0, h, 0, 0)),
        ] + cast_specs,
        out_specs=[pl.BlockSpec((None, tq, hp * MLA_V_DIM), lambda bi, h, i: (bi, i, h))]
        + cast_specs,
        out_shape=[jax.ShapeDtypeStruct((b, s, MLA_HEADS * MLA_V_DIM), BF16)]
        + [jax.ShapeDtypeStruct(w.shape, BF16) for w in cast_weights],
        scratch_shapes=[pltpu.VMEM((1, tq), F32), pltpu.VMEM((rows, tq), F32)] * hp
        + [pltpu.VMEM((tk, ATTN_GW), F32)] * SCORE_RING,
        compiler_params=pltpu.CompilerParams(
            dimension_semantics=("arbitrary", "arbitrary", "arbitrary"),
            vmem_limit_bytes=VMEM_LIMIT_BYTES),
        name="mla_attn",
    )(qt, k, vt, *cast_weights)
    return outs[0], outs[1:]


def _diff_attn_kernel(slopes_ref, qt_ref, k_ref, vt_ref, posq_ref, posk_ref,
                      lq1_ref, lk1_ref, lq2_ref, lk2_ref, sg_ref, o_ref, *scratch,
                      tq, tk, lambda_init):
    hp = pl.program_id(1)
    qi = pl.program_id(2)
    n_piece = tk // PROJ_TM
    n_group = tq // ATTN_GW
    n_full = qi * (tq // tk)
    n_maps = 2 * HEADS_PER_STEP
    stats = [[scratch[2 * (2 * hh + mi):2 * (2 * hh + mi) + 2] for mi in range(2)]
             for hh in range(HEADS_PER_STEP)]
    s_refs = scratch[2 * n_maps:]
    slopes = [slopes_ref[hp * HEADS_PER_STEP + hh] * LOG2E for hh in range(HEADS_PER_STEP)]
    for per_head in stats:
        for m_ref, acc_ref in per_head:
            _init_stats(m_ref, acc_ref)
    hd = 2 * DIFF_QK_DIM

    def make_fns():
        dist_cache, bias_cache = {}, {}

        def score(unit, s_ref):
            ckey, c, g, hh, mi, _ = unit
            rows = pl.ds(pl.multiple_of(c * tk, tk), tk)
            if (ckey, g) not in dist_cache:
                posk = posk_ref[rows, :]
                posq = posq_ref[:, g * ATTN_GW:(g + 1) * ATTN_GW]
                dist_cache[(ckey, g)] = jnp.abs(posk - posq)
            if (ckey, g, hh) not in bias_cache:
                bias_cache[(ckey, g, hh)] = slopes[hh] * dist_cache[(ckey, g)]
            bias = bias_cache[(ckey, g, hh)]
            col0 = hh * hd + mi * DIFF_QK_DIM
            kc = k_ref[rows, col0:col0 + DIFF_QK_DIM]
            for half in range(GROUP_PIECES):
                lanes = slice(half * PROJ_TM, (half + 1) * PROJ_TM)
                s_ref[:, lanes] = (_dot(kc, qt_ref[g * GROUP_PIECES + half, hh, mi])
                                   - bias[:, lanes])

        def consume(unit, s):
            _, c, g, hh, mi, key_off = unit
            if key_off is not None:
                s = _causal_mask_t(s, g, key_off)
            m_ref, acc_ref = stats[hh][mi]
            vts = [vt_ref[c * n_piece + u, hh] for u in range(n_piece)]
            _softmax_step_t(s, vts, m_ref, acc_ref, slice(g * ATTN_GW, (g + 1) * ATTN_GW))

        return score, consume

    def chunk_units(ckey, c):
        return [(ckey, c, g, hh, mi, None) for g in range(n_group)
                for hh in range(HEADS_PER_STEP) for mi in range(2)]

    assert (n_group * n_maps) % len(s_refs) == 0

    def body(c, carry):
        score, consume = make_fns()
        _run_pipelined(chunk_units("cur", c), s_refs, score, consume,
                       next_units=chunk_units("next", c + 1))
        return carry

    score, consume = make_fns()
    _prime_pipeline(chunk_units("first", 0), s_refs, score)
    lax.fori_loop(0, n_full, body, 0)
    diag = [(j, n_full + j, g, hh, mi, off) for j, g, off in _diag_units(tq, tk)
            for hh in range(HEADS_PER_STEP) for mi in range(2)]
    _run_pipelined(diag, s_refs, score, consume)

    lam = (jnp.exp(jnp.sum(lq1_ref[...] * lk1_ref[...], axis=-1, keepdims=True))
           - jnp.exp(jnp.sum(lq2_ref[...] * lk2_ref[...], axis=-1, keepdims=True))
           + lambda_init)
    for hh, ((_, acc0_ref), (_, acc1_ref)) in enumerate(stats):
        o_t = _normalized(acc0_ref, DIFF_V_DIM) - lam * _normalized(acc1_ref, DIFF_V_DIM)
        o = _rms_norm_rows(o_t.T, sg_ref[...]) * (1.0 - lambda_init)
        o_ref[:, hh * DIFF_V_DIM:(hh + 1) * DIFF_V_DIM] = o.astype(o_ref.dtype)


def _diff_attn(slopes, dqt, dk, dvt, positions, lq1, lk1, lq2, lk2, sg, *, lambda_init):
    b, s, _ = dk.shape
    tq, tk = ATTN_TQ, ATTN_TK
    hp = HEADS_PER_STEP
    hd = 2 * DIFF_QK_DIM
    rows = DIFF_V_DIM + ONES_ROWS
    posf = positions.astype(F32)
    posq = posf.reshape(b, 1, s)
    posk = posf.reshape(b, s, 1)
    vec = lambda n: pl.BlockSpec((1, n), lambda bi, h, i: (0, 0))
    return pl.pallas_call(
        functools.partial(_diff_attn_kernel, tq=tq, tk=tk, lambda_init=lambda_init),
        grid=(b, DIFF_HEADS // hp, s // tq),
        in_specs=[
            pl.BlockSpec(memory_space=pltpu.SMEM),
            pl.BlockSpec((None, tq // PROJ_TM, hp, 2, DIFF_QK_DIM, PROJ_TM),
                         lambda bi, h, i: (bi, i, h, 0, 0, 0)),
            pl.BlockSpec((None, s, hp * hd), lambda bi, h, i: (bi, 0, h)),
            pl.BlockSpec((None, s // PROJ_TM, hp, rows, PROJ_TM),
                         lambda bi, h, i: (bi, 0, h, 0, 0)),
            pl.BlockSpec((None, 1, tq), lambda bi, h, i: (bi, 0, i)),
            pl.BlockSpec((None, s, 1), lambda bi, h, i: (bi, 0, 0)),
            vec(DIFF_QK_DIM), vec(DIFF_QK_DIM), vec(DIFF_QK_DIM), vec(DIFF_QK_DIM),
            vec(DIFF_V_DIM),
        ],
        out_specs=pl.BlockSpec((None, tq, hp * DIFF_V_DIM), lambda bi, h, i: (bi, i, h)),
        out_shape=jax.ShapeDtypeStruct((b, s, DIFF_V_COLS), BF16),
        scratch_shapes=[pltpu.VMEM((1, tq), F32), pltpu.VMEM((rows, tq), F32)] * (2 * hp)
        + [pltpu.VMEM((tk, ATTN_GW), F32)] * SCORE_RING,
        compiler_params=pltpu.CompilerParams(
            dimension_semantics=("parallel", "parallel", "arbitrary"),
            vmem_limit_bytes=VMEM_LIMIT_BYTES),
        name="diff_attn",
    )(slopes, dqt, dk, dvt, posq, posk, lq1, lk1, lq2, lk2, sg)


def _merge_ln_kernel(h_ref, om_ref, od_ref, wgm_ref, wgd_ref, wbm_ref, wbd_ref, wo_ref,
                     g_ref, b_ref, *rest, tn):
    n_cast = (len(rest) - 2) // 2
    o_ref, y_ref = rest[n_cast], rest[-1]
    _cast_slabs(rest[:n_cast] + rest[n_cast + 1:-1])
    h = h_ref[...]
    hb = h.astype(BF16)
    om = om_ref[...]
    od = od_ref[...]
    for c in range(D_MODEL // tn):
        cols = slice(c * tn, (c + 1) * tn)
        gm = jax.nn.sigmoid(_dot(hb, wgm_ref[:, cols]))
        gd = jax.nn.sigmoid(_dot(hb, wgd_ref[:, cols]))
        y = gm * _dot(om, wbm_ref[:, cols]) + gd * _dot(od, wbd_ref[:, cols])
        y_ref[:, cols] = y.astype(BF16)
    mix = _dot(y_ref[...], wo_ref[...])
    o_ref[...] = _layer_norm_rows(ALPHA * h + mix, g_ref[...], b_ref[...])


def _merge_ln(h, om, od, wgm, wgd, wbm, wbd, wo, g, b, cast_weights, *, tm=256, tn=512):
    t, d = h.shape
    steps = t // tm
    row = lambda n: pl.BlockSpec((tm, n), lambda i: (i, 0))
    cast_specs = [_cast_slab_spec(w, steps) for w in cast_weights]
    outs = pl.pallas_call(
        functools.partial(_merge_ln_kernel, tn=tn),
        grid=(steps,),
        in_specs=[row(d), row(om.shape[1]), row(od.shape[1]),
                  _resident(wgm.shape), _resident(wgd.shape), _resident(wbm.shape),
                  _resident(wbd.shape), _resident(wo.shape),
                  _resident(g.shape), _resident(b.shape)] + cast_specs,
        out_specs=[row(d)] + cast_specs,
        out_shape=[jax.ShapeDtypeStruct((t, d), F32)]
        + [jax.ShapeDtypeStruct(w.shape, BF16) for w in cast_weights],
        scratch_shapes=[pltpu.VMEM((tm, d), BF16)],
        compiler_params=pltpu.CompilerParams(
            dimension_semantics=("arbitrary",), vmem_limit_bytes=VMEM_LIMIT_BYTES),
        name="merge_ln",
    )(h, om, od, wgm, wgd, wbm, wbd, wo, g, b, *cast_weights)
    return outs[0], outs[1:]


W_IN_PREP_STEPS = 8
W_IN_N_TRANSPOSED = 4


def _w_in_prep_kernel(*refs):
    n = len(refs) // 2
    for k, (src, dst) in enumerate(zip(refs[:n], refs[n:])):
        w = src[...]
        dst[...] = (w.T if k < W_IN_N_TRANSPOSED else w).astype(BF16)


def _prep_w_in(w_in_t):
    d = w_in_t.shape[1]
    steps = W_IN_PREP_STEPS
    base = MLA_Q_RANK + MLA_KV_RANK
    d0 = base + MLA_ROPE_DIM
    v0 = d0 + 2 * DIFF_QK_COLS
    g0 = v0 + DIFF_V_COLS
    pieces = [(0, base), (d0 + DIFF_QK_COLS, DIFF_QK_COLS), (g0, D_MODEL), (g0 + D_MODEL, D_MODEL),
              (d0, DIFF_QK_COLS), (v0, DIFF_V_COLS)]
    in_specs, out_specs, out_shapes = [], [], []
    for k, (r0, rows) in enumerate(pieces):
        per = rows // steps
        in_specs.append(pl.BlockSpec((pl.Element(per), pl.Element(d)),
                                     lambda i, r0=r0, per=per: (pl.multiple_of(r0 + i * per, 64), 0)))
        if k < W_IN_N_TRANSPOSED:
            out_specs.append(pl.BlockSpec((d, per), lambda i: (0, i)))
            out_shapes.append(jax.ShapeDtypeStruct((d, rows), BF16))
        else:
            out_specs.append(pl.BlockSpec((per, d), lambda i: (i, 0)))
            out_shapes.append(jax.ShapeDtypeStruct((rows, d), BF16))
    in_specs.append(pl.BlockSpec((MLA_ROPE_DIM, d), lambda i: (base // MLA_ROPE_DIM, 0)))
    out_specs.append(pl.BlockSpec((MLA_ROPE_DIM, d), lambda i: (0, 0)))
    out_shapes.append(jax.ShapeDtypeStruct((MLA_ROPE_DIM, d), BF16))
    return pl.pallas_call(
        _w_in_prep_kernel,
        grid=(steps,),
        in_specs=in_specs,
        out_specs=out_specs,
        out_shape=out_shapes,
        compiler_params=pltpu.CompilerParams(
            dimension_semantics=("arbitrary",), vmem_limit_bytes=VMEM_LIMIT_BYTES),
        name="w_in_prep",
    )(*([w_in_t] * len(in_specs)))


def _prep_w_uq_t(w_uq):
    r = w_uq.shape[0]
    wq3 = w_uq.reshape(r, MLA_HEADS, MLA_QK_DIM)
    z3 = jnp.zeros((r, MLA_HEADS, MLA_QK_PAD - MLA_QK_DIM), w_uq.dtype)
    return jnp.concatenate([wq3, z3], axis=-1).reshape(r, MLA_HEADS * MLA_QK_PAD).T


def kernel(x, positions, ln1_g, ln1_b, ffn1_w_gate, ffn1_w_up, ffn1_w_down, w_in, mla_q_norm_g, mla_w_uq, mla_kv_norm_g, mla_w_uk, mla_w_uv, diff_lambda_q1, diff_lambda_k1, diff_lambda_q2, diff_lambda_k2, diff_subln_g, w_branch_mla, w_branch_diff, w_out, ln2_g, ln2_b, ffn2_w_gate, ffn2_w_up, ffn2_w_down, ln3_g, ln3_b):
    b, s, d = x.shape
    t = b * s
    bf = lambda w: w.astype(BF16)

    half = MLA_ROPE_DIM // 2
    inv_freq = ROPE_THETA ** (-jnp.arange(half, dtype=F32) / half)
    invf = inv_freq[:, None]
    slopes = 2.0 ** (-8.0 * jnp.arange(1, DIFF_HEADS + 1, dtype=F32) / DIFF_HEADS)

    h = x.reshape(t, d)
    for l in range(DEPTH):
        lambda_init = 0.8 - 0.6 * math.exp(-0.3 * l)
        head, w1g, w1u, w1d = _ffn_ln_head(h, ffn1_w_gate[l], ffn1_w_up[l], ffn1_w_down[l],
                                           ln1_g[l][None, :], ln1_b[l][None, :])
        h = _ffn_ln(h, w1g, w1u, w1d, ln1_g[l][None, :], ln1_b[l][None, :], head_tile=head)

        wlat, wdk, wgm, wgd, wdqt, wdvt, wkrt = _prep_w_in(w_in[l].T)
        wqt = _prep_w_uq_t(bf(mla_w_uq[l]))
        pieces = s // PROJ_TM
        qt, k, vt, dqt, dk, dvt = _mixer_proj(
            h, positions.reshape(t // PROJ_TM, 1, PROJ_TM).astype(F32), invf,
            wlat, wkrt, wdqt, wdk, wdvt, mla_q_norm_g[l][None, :], mla_kv_norm_g[l][None, :],
            wqt, bf(mla_w_uk[l]), bf(mla_w_uv[l]).T)

        o_mla, (wbm, wbd, wo) = _mla_attn(
            qt.reshape((b, pieces) + qt.shape[1:]), k.reshape(b, s, -1),
            vt.reshape((b, pieces) + vt.shape[1:]),
            (w_branch_mla[l], w_branch_diff[l], w_out[l]))
        o_diff = _diff_attn(
            slopes, dqt.reshape(b, pieces, DIFF_HEADS, 2, DIFF_QK_DIM, PROJ_TM),
            dk.reshape(b, s, -1), dvt.reshape((b, pieces) + dvt.shape[1:]), positions,
            diff_lambda_q1[l][None, :], diff_lambda_k1[l][None, :],
            diff_lambda_q2[l][None, :], diff_lambda_k2[l][None, :],
            diff_subln_g[l][None, :], lambda_init=lambda_init)

        h, (w2g, w2u, w2d) = _merge_ln(
            h, o_mla.reshape(t, -1), o_diff.reshape(t, -1), wgm, wgd, wbm, wbd, wo,
            ln2_g[l][None, :], ln2_b[l][None, :],
            (ffn2_w_gate[l], ffn2_w_up[l], ffn2_w_down[l]))

        h = _ffn_ln(h, w2g, w2u, w2d, ln3_g[l][None, :], ln3_b[l][None, :])
    return h.reshape(b, s, d)
```

```python
import functools
import math

import jax
import jax.numpy as jnp
from jax import lax
from jax.experimental import pallas as pl
from jax.experimental.pallas import tpu as pltpu

D_MODEL = 2048
DEPTH = 1
MLA_HEADS = 8
MLA_Q_RANK = 512
MLA_KV_RANK = 512
MLA_NOPE_DIM = 128
MLA_ROPE_DIM = 64
MLA_V_DIM = 128
MLA_QK_DIM = MLA_NOPE_DIM + MLA_ROPE_DIM
MLA_QK_PAD = 256
ROPE_THETA = 10000.0
DIFF_HEADS = 4
DIFF_QK_DIM = 128
DIFF_V_DIM = 2 * DIFF_QK_DIM
DIFF_QK_COLS = DIFF_HEADS * 2 * DIFF_QK_DIM
DIFF_V_COLS = DIFF_HEADS * DIFF_V_DIM
D_FF = 5632
LN_EPS = 1e-5
RMS_EPS = 1e-6
ALPHA = (2 * DEPTH) ** 0.25
LOG2E = math.log2(math.e)

LANE = 128
MXU_DIM = 256
ONES_ROWS = 16
VMEM_LIMIT_BYTES = 56 * 1024 * 1024
VMEM_INTERNAL_SCRATCH_BYTES = 8 * 1024 * 1024
PROJ_TM = MXU_DIM
ATTN_TQ = 2048
ATTN_TK = 512
ATTN_GW = 2 * MXU_DIM
GROUP_PIECES = ATTN_GW // PROJ_TM
LOOKAHEAD = 2
SCORE_RING = 4
MLA_HEADS_PER_STEP = 4
DIFF_HEADS_PER_STEP = 2

F32 = jnp.float32
BF16 = jnp.bfloat16


def _dot(a, b):
    return jnp.dot(a, b, preferred_element_type=F32)


def _dot_nt(a, b):
    return lax.dot_general(a, b, (((1,), (1,)), ((), ())), preferred_element_type=F32)


def _layer_norm_rows(y, g, b):
    mu = jnp.mean(y, axis=-1, keepdims=True)
    d = y - mu
    var = jnp.mean(d * d, axis=-1, keepdims=True)
    return d * lax.rsqrt(var + LN_EPS) * g + b


def _rms_norm_rows(y, g):
    ms = jnp.mean(y * y, axis=-1, keepdims=True)
    return y * lax.rsqrt(ms + RMS_EPS) * g


def _resident(shape):
    return pl.BlockSpec(shape, lambda *_: (0,) * len(shape), pipeline_mode=pl.Buffered(1))


def _cast_slab_spec(w, n_steps, step_of=lambda *idx: idx[0]):
    rows, cols = w.shape
    slabs = n_steps
    while rows % slabs or (rows // slabs) % 16:
        slabs //= 2
    per = n_steps // slabs
    return pl.BlockSpec((rows // slabs, cols), lambda *idx: (step_of(*idx) // per, 0))


def _cast_slabs(refs):
    n = len(refs) // 2
    for src, dst in zip(refs[:n], refs[n:]):
        dst[...] = src[...].astype(BF16)


def _ffn_step(x_ref, wg_ref, wu_ref, wd_ref, g_ref, b_ref, o_ref, xb_ref):
    j = pl.program_id(1)

    @pl.when(j == 0)
    def _():
        x = x_ref[...]
        xb_ref[...] = x.astype(BF16)
        o_ref[...] = ALPHA * x

    xb = xb_ref[...]
    gate = _dot(xb, wg_ref[...])
    up = _dot(xb, wu_ref[...])
    act = (0.5 * (gate * jax.nn.sigmoid(gate))) * up
    o_ref[...] += _dot(act.astype(BF16), wd_ref[...])

    @pl.when(j == pl.num_programs(1) - 1)
    def _():
        o_ref[...] = _layer_norm_rows(o_ref[...], g_ref[...], b_ref[...])


def _ffn_ln_kernel(x_ref, wg_ref, wu_ref, wd_ref, g_ref, b_ref, o_ref, xb_ref):
    _ffn_step(x_ref, wg_ref, wu_ref, wd_ref, g_ref, b_ref, o_ref, xb_ref)


def _ffn_ln_tail_kernel(x_ref, wg_ref, wu_ref, wd_ref, g_ref, b_ref, head_ref,
                        o_ref, xb_ref, sem):
    i = pl.program_id(0)

    @pl.when(jnp.logical_and(i == 0, pl.program_id(1) == 0))
    def _():
        copy = pltpu.make_async_copy(head_ref, o_ref, sem)
        copy.start()
        copy.wait()

    @pl.when(i > 0)
    def _():
        _ffn_step(x_ref, wg_ref, wu_ref, wd_ref, g_ref, b_ref, o_ref, xb_ref)


def _ffn_ln_head_kernel(x_ref, wg_ref, wu_ref, wd_ref, g_ref, b_ref,
                        o_ref, wgb_ref, wub_ref, wdb_ref, xb_ref):
    _cast_slabs((wg_ref, wu_ref, wd_ref, wgb_ref, wub_ref, wdb_ref))
    _ffn_step(x_ref, wgb_ref, wub_ref, wdb_ref, g_ref, b_ref, o_ref, xb_ref)


def _ffn_specs(tm, tf, d, skip_first_tile=False):
    col = (lambda i, j: jnp.where(i == 0, 0, j)) if skip_first_tile else (lambda i, j: j)
    return [
        pl.BlockSpec((tm, d), lambda i, j: (i, 0)),
        pl.BlockSpec((d, tf), lambda i, j: (0, col(i, j))),
        pl.BlockSpec((d, tf), lambda i, j: (0, col(i, j))),
        pl.BlockSpec((tf, d), lambda i, j: (col(i, j), 0)),
        pl.BlockSpec((1, d), lambda i, j: (0, 0)),
        pl.BlockSpec((1, d), lambda i, j: (0, 0)),
    ]


def _ffn_ln_head(x, wg, wu, wd, g, b, *, tm=1024, tf=256):
    d = x.shape[1]
    f = wg.shape[1]
    specs = _ffn_specs(tm, tf, d)
    vmem_bytes = ((2 * tm * d * 4) * 2 + tm * d * 2 + 2 * (3 * d * tf * (4 + 2))
                  + tm * tf * (4 + 4 + 2))
    return pl.pallas_call(
        _ffn_ln_head_kernel,
        grid=(1, f // tf),
        in_specs=specs,
        out_specs=[specs[0]] + specs[1:4],
        out_shape=[jax.ShapeDtypeStruct((tm, d), F32)]
        + [jax.ShapeDtypeStruct(w.shape, BF16) for w in (wg, wu, wd)],
        scratch_shapes=[pltpu.VMEM((tm, d), BF16)],
        compiler_params=pltpu.CompilerParams(
            dimension_semantics=("arbitrary", "arbitrary"),
            vmem_limit_bytes=vmem_bytes + VMEM_INTERNAL_SCRATCH_BYTES),
        name="ffn_ln_head",
    )(x, wg, wu, wd, g, b)


def _ffn_ln(x, wg, wu, wd, g, b, *, head_tile=None, tm=1024, tf=512):
    t, d = x.shape
    f = wg.shape[1]
    with_head = head_tile is not None
    specs = _ffn_specs(tm, tf, d, skip_first_tile=with_head)
    vmem_bytes = (2 * tm * d * 4) * 2 + tm * d * 2 + 2 * (3 * d * tf * 2) + tm * tf * (4 + 4 + 2)
    return pl.pallas_call(
        _ffn_ln_tail_kernel if with_head else _ffn_ln_kernel,
        grid=(t // tm, f // tf),
        in_specs=specs + ([pl.BlockSpec(memory_space=pl.ANY)] if with_head else []),
        out_specs=specs[0],
        out_shape=jax.ShapeDtypeStruct((t, d), F32),
        scratch_shapes=[pltpu.VMEM((tm, d), BF16)]
        + ([pltpu.SemaphoreType.DMA(())] if with_head else []),
        compiler_params=pltpu.CompilerParams(
            dimension_semantics=("arbitrary", "arbitrary"),
            vmem_limit_bytes=vmem_bytes + VMEM_INTERNAL_SCRATCH_BYTES),
        name="ffn_ln",
    )(x, wg, wu, wd, g, b, *((head_tile,) if with_head else ()))


def _mixer_proj_kernel(h_ref, pos_ref, invf_ref, wlat_ref, wkrt_ref, wdqt_ref, wdk_ref, wdvt_ref,
                       qg_ref, kvg_ref, wqt_ref, wuk_ref, wuvt_ref,
                       qt_ref, k_ref, vt_ref, dqt_ref, dk_ref, dvt_ref):
    hb = h_ref[...].astype(BF16)
    tm = hb.shape[0]

    dq_scale = DIFF_QK_DIM ** -0.5 * LOG2E
    dqt_ref[...] = (_dot_nt(wdqt_ref[...], hb) * dq_scale).astype(BF16)
    dk_ref[...] = _dot(hb, wdk_ref[...]).astype(BF16)
    ones_rows = (lax.broadcasted_iota(jnp.int32, (ONES_ROWS, tm), 0) == 0).astype(BF16)
    dvt = _dot_nt(wdvt_ref[...], hb).astype(BF16)
    for h in range(DIFF_HEADS):
        dvt_ref[h, 0:DIFF_V_DIM, :] = dvt[h * DIFF_V_DIM:(h + 1) * DIFF_V_DIM, :]
        dvt_ref[h, DIFF_V_DIM:, :] = ones_rows

    lat = _dot(hb, wlat_ref[...])
    c_q = lat[:, 0:MLA_Q_RANK]
    c_kv = lat[:, MLA_Q_RANK:MLA_Q_RANK + MLA_KV_RANK]
    k_r_t = _dot_nt(wkrt_ref[...], hb)

    ang = invf_ref[...] * pos_ref[...]
    cos_f, sin_f = jnp.cos(ang), jnp.sin(ang)
    half = MLA_ROPE_DIM // 2
    pad = jnp.zeros((LANE - MLA_ROPE_DIM, tm), F32)

    def rope_t(r):
        x1, x2 = r[0:half, :], r[half:2 * half, :]
        return jnp.concatenate([x1 * cos_f - x2 * sin_f, x1 * sin_f + x2 * cos_f, pad], axis=0)

    cqn = _rms_norm_rows(c_q, qg_ref[...]).astype(BF16)
    ckvn = _rms_norm_rows(c_kv, kvg_ref[...]).astype(BF16)

    q_t = _dot_nt(wqt_ref[...], cqn)
    k_nope = _dot(ckvn, wuk_ref[...])
    vt = _dot_nt(wuvt_ref[...], ckvn).astype(BF16)
    for h in range(MLA_HEADS):
        vt_ref[h, 0:MLA_V_DIM, :] = vt[h * MLA_V_DIM:(h + 1) * MLA_V_DIM, :]
        vt_ref[h, MLA_V_DIM:, :] = ones_rows

    q_scale = MLA_QK_DIM ** -0.5 * LOG2E
    k_rope = rope_t(k_r_t).T.astype(BF16)
    for h in range(MLA_HEADS):
        c0 = h * MLA_QK_PAD
        qt_ref[h, 0:LANE, :] = (q_t[c0:c0 + LANE, :] * q_scale).astype(BF16)
        q_rope = rope_t(q_t[c0 + LANE:c0 + LANE + MLA_ROPE_DIM, :])
        qt_ref[h, LANE:2 * LANE, :] = (q_rope * q_scale).astype(BF16)
        k_ref[:, c0:c0 + LANE] = k_nope[:, h * LANE:(h + 1) * LANE].astype(BF16)
        k_ref[:, c0 + LANE:c0 + 2 * LANE] = k_rope


def _mixer_proj(h, pos, invf, wlat, wkrt, wdqt, wdk, wdvt, qg, kvg, wqt, wuk, wuvt):
    t, d = h.shape
    tm = PROJ_TM
    qk_cols = MLA_HEADS * MLA_QK_PAD
    row = lambda n: pl.BlockSpec((tm, n), lambda i: (i, 0))
    tile_t = lambda *dims: pl.BlockSpec((None,) + dims + (tm,),
                                        lambda i: (i,) + (0,) * (len(dims) + 1))
    rows_out = lambda n: jax.ShapeDtypeStruct((t, n), BF16)
    tile_t_out = lambda *dims: jax.ShapeDtypeStruct((t // tm,) + dims + (tm,), BF16)
    weights = (invf, wlat, wkrt, wdqt, wdk, wdvt, qg, kvg, wqt, wuk, wuvt)
    return pl.pallas_call(
        _mixer_proj_kernel,
        grid=(t // tm,),
        in_specs=[row(d), tile_t(1)] + [_resident(w.shape) for w in weights],
        out_specs=[tile_t(MLA_HEADS, MLA_QK_PAD), row(qk_cols),
                   tile_t(MLA_HEADS, MLA_V_DIM + ONES_ROWS),
                   tile_t(DIFF_QK_COLS), row(DIFF_QK_COLS),
                   tile_t(DIFF_HEADS, DIFF_V_DIM + ONES_ROWS)],
        out_shape=[tile_t_out(MLA_HEADS, MLA_QK_PAD), rows_out(qk_cols),
                   tile_t_out(MLA_HEADS, MLA_V_DIM + ONES_ROWS),
                   tile_t_out(DIFF_QK_COLS), rows_out(DIFF_QK_COLS),
                   tile_t_out(DIFF_HEADS, DIFF_V_DIM + ONES_ROWS)],
        compiler_params=pltpu.CompilerParams(
            dimension_semantics=("parallel",), vmem_limit_bytes=VMEM_LIMIT_BYTES),
        name="mixer_proj",
    )(h, pos, *weights)


def _diag_units(tq, tk):
    units = []
    for j in range(tq // tk):
        for g in range(tq // ATTN_GW):
            if j * tk >= (g + 1) * ATTN_GW:
                continue
            visible = (j + 1) * tk <= g * ATTN_GW + 1
            units.append((j, g, None if visible else j * tk))
    return units


def _causal_mask_t(s, g, key_off):
    rows = lax.broadcasted_iota(jnp.int32, s.shape, 0)
    cols = lax.broadcasted_iota(jnp.int32, s.shape, 1)
    return jnp.where(rows + key_off <= cols + g * ATTN_GW, s, -jnp.inf)


def _run_pipelined(units, s_refs, score_fn, consume_fn, next_units=()):
    ring = len(s_refs)
    todo = list(enumerate(list(units) + list(next_units)))[LOOKAHEAD:]
    for k, u in enumerate(units):
        if todo:
            kk, nu = todo.pop(0)
            score_fn(nu, s_refs[kk % ring])
        consume_fn(u, s_refs[k % ring][...])


def _prime_pipeline(units, s_refs, score_fn):
    for k, u in enumerate(units[:LOOKAHEAD]):
        score_fn(u, s_refs[k])


def _softmax_step_t(s, vt_pieces, m_ref, acc_ref, cols):
    m_prev = m_ref[:, cols]
    m_new = jnp.maximum(m_prev, jnp.max(s, axis=0, keepdims=True))
    corr = jnp.exp2(m_prev - m_new)
    p = jnp.exp2(s - m_new).astype(BF16)
    pv = None
    for u, vt in enumerate(vt_pieces):
        part = _dot(vt, p[u * PROJ_TM:(u + 1) * PROJ_TM, :])
        pv = part if pv is None else pv + part
    acc_ref[:, cols] = corr * acc_ref[:, cols] + pv
    m_ref[:, cols] = m_new


def _init_stats(m_ref, acc_ref):
    m_ref[...] = jnp.full(m_ref.shape, -jnp.inf, F32)
    acc_ref[...] = jnp.zeros(acc_ref.shape, F32)


def _normalized(acc_ref, dv):
    return acc_ref[0:dv, :] * (1.0 / acc_ref[dv:dv + 1, :])


def _mla_attn_kernel(qt_ref, k_ref, vt_ref, *rest, tq, tk, n_cast):
    cast_in, o_ref, cast_out = rest[:n_cast], rest[n_cast], rest[n_cast + 1:2 * n_cast + 1]
    scratch = rest[2 * n_cast + 1:]
    n_heads = MLA_HEADS_PER_STEP
    stats = [scratch[2 * hh:2 * hh + 2] for hh in range(n_heads)]
    s_refs = scratch[2 * n_heads:]
    _cast_slabs(cast_in + cast_out)
    qi = pl.program_id(2)
    n_piece = tk // PROJ_TM
    n_group = tq // ATTN_GW
    n_full = qi * (tq // tk)
    for m_ref, acc_ref in stats:
        _init_stats(m_ref, acc_ref)

    def score(unit, s_ref):
        hh, c, g, _ = unit
        kc = k_ref[pl.ds(pl.multiple_of(c * tk, tk), tk), hh * MLA_QK_PAD:(hh + 1) * MLA_QK_PAD]
        for half in range(GROUP_PIECES):
            s_ref[:, half * PROJ_TM:(half + 1) * PROJ_TM] = _dot(
                kc, qt_ref[g * GROUP_PIECES + half, hh])

    def consume(unit, s):
        hh, c, g, key_off = unit
        if key_off is not None:
            s = _causal_mask_t(s, g, key_off)
        m_ref, acc_ref = stats[hh]
        vts = [vt_ref[c * n_piece + u, hh] for u in range(n_piece)]
        _softmax_step_t(s, vts, m_ref, acc_ref, slice(g * ATTN_GW, (g + 1) * ATTN_GW))

    def chunk_units(c):
        return [(hh, c, g, None) for g in range(n_group) for hh in range(n_heads)]

    assert (n_group * n_heads) % len(s_refs) == 0

    def body(c, carry):
        _run_pipelined(chunk_units(c), s_refs, score, consume, next_units=chunk_units(c + 1))
        return carry

    _prime_pipeline(chunk_units(0), s_refs, score)
    lax.fori_loop(0, n_full, body, 0)
    diag = [(hh, n_full + j, g, off) for j, g, off in _diag_units(tq, tk)
            for hh in range(n_heads)]
    _run_pipelined(diag, s_refs, score, consume)
    for hh, (_, acc_ref) in enumerate(stats):
        o_ref[:, hh * MLA_V_DIM:(hh + 1) * MLA_V_DIM] = (
            _normalized(acc_ref, MLA_V_DIM).T.astype(o_ref.dtype))


def _mla_attn(qt, k, vt, cast_weights):
    b, s, _ = k.shape
    tq, tk = ATTN_TQ, ATTN_TK
    rows = MLA_V_DIM + ONES_ROWS
    nq = s // tq
    hp = MLA_HEADS_PER_STEP
    n_hp = MLA_HEADS // hp
    steps = b * n_hp * nq
    cast_specs = [_cast_slab_spec(w, steps, lambda bi, h, i: (bi * n_hp + h) * nq + i)
                  for w in cast_weights]
    outs = pl.pallas_call(
        functools.partial(_mla_attn_kernel, tq=tq, tk=tk, n_cast=len(cast_weights)),
        grid=(b, n_hp, nq),
        in_specs=[
            pl.BlockSpec((None, tq // PROJ_TM, hp, MLA_QK_PAD, PROJ_TM),
                         lambda bi, h, i: (bi, i, h, 0, 0)),
            pl.BlockSpec((None, s, hp * MLA_QK_PAD), lambda bi, h, i: (bi, 0, h)),
            pl.BlockSpec((None, s // PROJ_TM, hp, rows, PROJ_TM),
                         lambda bi, h, i: (bi, 0, h, 0, 0)),
        ] + cast_specs,
        out_specs=[pl.BlockSpec((None, tq, hp * MLA_V_DIM), lambda bi, h, i: (bi, i, h))]
        + cast_specs,
        out_shape=[jax.ShapeDtypeStruct((b, s, MLA_HEADS * MLA_V_DIM), BF16)]
        + [jax.ShapeDtypeStruct(w.shape, BF16) for w in cast_weights],
        scratch_shapes=[pltpu.VMEM((1, tq), F32), pltpu.VMEM((rows, tq), F32)] * hp
        + [pltpu.VMEM((tk, ATTN_GW), F32)] * SCORE_RING,
        compiler_params=pltpu.CompilerParams(
            dimension_semantics=("arbitrary", "arbitrary", "arbitrary"),
            vmem_limit_bytes=VMEM_LIMIT_BYTES),
        name="mla_attn",
    )(qt, k, vt, *cast_weights)
    return outs[0], outs[1:]


def _diff_attn_kernel(slopes_ref, qt_ref, k_ref, vt_ref, posq_ref, posk_ref,
                      lq1_ref, lk1_ref, lq2_ref, lk2_ref, sg_ref, o_ref, *scratch,
                      tq, tk, lambda_init):
    hp = pl.program_id(1)
    qi = pl.program_id(2)
    n_piece = tk // PROJ_TM
    n_group = tq // ATTN_GW
    n_full = qi * (tq // tk)
    n_heads = DIFF_HEADS_PER_STEP
    n_maps = 2 * n_heads
    stats = [[scratch[2 * (2 * hh + mi):2 * (2 * hh + mi) + 2] for mi in range(2)]
             for hh in range(n_heads)]
    s_refs = scratch[2 * n_maps:]
    slopes = [slopes_ref[hp * n_heads + hh] * LOG2E for hh in range(n_heads)]
    for per_head in stats:
        for m_ref, acc_ref in per_head:
            _init_stats(m_ref, acc_ref)
    hd = 2 * DIFF_QK_DIM

    def make_fns():
        dist_cache, bias_cache = {}, {}

        def score(unit, s_ref):
            ckey, c, g, hh, mi, _ = unit
            rows = pl.ds(pl.multiple_of(c * tk, tk), tk)
            if (ckey, g) not in dist_cache:
                posk = posk_ref[rows, :]
                posq = posq_ref[:, g * ATTN_GW:(g + 1) * ATTN_GW]
                dist_cache[(ckey, g)] = jnp.abs(posk - posq)
            if (ckey, g, hh) not in bias_cache:
                bias_cache[(ckey, g, hh)] = slopes[hh] * dist_cache[(ckey, g)]
            bias = bias_cache[(ckey, g, hh)]
            col0 = hh * hd + mi * DIFF_QK_DIM
            kc = k_ref[rows, col0:col0 + DIFF_QK_DIM]
            for half in range(GROUP_PIECES):
                lanes = slice(half * PROJ_TM, (half + 1) * PROJ_TM)
                s_ref[:, lanes] = (_dot(kc, qt_ref[g * GROUP_PIECES + half, hh, mi])
                                   - bias[:, lanes])

        def consume(unit, s):
            _, c, g, hh, mi, key_off = unit
            if key_off is not None:
                s = _causal_mask_t(s, g, key_off)
            m_ref, acc_ref = stats[hh][mi]
            vts = [vt_ref[c * n_piece + u, hh] for u in range(n_piece)]
            _softmax_step_t(s, vts, m_ref, acc_ref, slice(g * ATTN_GW, (g + 1) * ATTN_GW))

        return score, consume

    def chunk_units(ckey, c):
        return [(ckey, c, g, hh, mi, None) for g in range(n_group)
                for hh in range(n_heads) for mi in range(2)]

    assert (n_group * n_maps) % len(s_refs) == 0

    def body(c, carry):
        score, consume = make_fns()
        _run_pipelined(chunk_units("cur", c), s_refs, score, consume,
                       next_units=chunk_units("next", c + 1))
        return carry

    score, consume = make_fns()
    _prime_pipeline(chunk_units("first", 0), s_refs, score)
    lax.fori_loop(0, n_full, body, 0)
    diag = [(j, n_full + j, g, hh, mi, off) for j, g, off in _diag_units(tq, tk)
            for hh in range(n_heads) for mi in range(2)]
    _run_pipelined(diag, s_refs, score, consume)

    lam = (jnp.exp(jnp.sum(lq1_ref[...] * lk1_ref[...], axis=-1, keepdims=True))
           - jnp.exp(jnp.sum(lq2_ref[...] * lk2_ref[...], axis=-1, keepdims=True))
           + lambda_init)
    for hh, ((_, acc0_ref), (_, acc1_ref)) in enumerate(stats):
        o_t = _normalized(acc0_ref, DIFF_V_DIM) - lam * _normalized(acc1_ref, DIFF_V_DIM)
        o = _rms_norm_rows(o_t.T, sg_ref[...]) * (1.0 - lambda_init)
        o_ref[:, hh * DIFF_V_DIM:(hh + 1) * DIFF_V_DIM] = o.astype(o_ref.dtype)


def _diff_attn(slopes, dqt, dk, dvt, positions, lq1, lk1, lq2, lk2, sg, *, lambda_init):
    b, s, _ = dk.shape
    tq, tk = ATTN_TQ, ATTN_TK
    hp = DIFF_HEADS_PER_STEP
    hd = 2 * DIFF_QK_DIM
    rows = DIFF_V_DIM + ONES_ROWS
    posf = positions.astype(F32)
    posq = posf.reshape(b, 1, s)
    posk = posf.reshape(b, s, 1)
    vec = lambda n: pl.BlockSpec((1, n), lambda bi, h, i: (0, 0))
    return pl.pallas_call(
        functools.partial(_diff_attn_kernel, tq=tq, tk=tk, lambda_init=lambda_init),
        grid=(b, DIFF_HEADS // hp, s // tq),
        in_specs=[
            pl.BlockSpec(memory_space=pltpu.SMEM),
            pl.BlockSpec((None, tq // PROJ_TM, hp, 2, DIFF_QK_DIM, PROJ_TM),
                         lambda bi, h, i: (bi, i, h, 0, 0, 0)),
            pl.BlockSpec((None, s, hp * hd), lambda bi, h, i: (bi, 0, h)),
            pl.BlockSpec((None, s // PROJ_TM, hp, rows, PROJ_TM),
                         lambda bi, h, i: (bi, 0, h, 0, 0)),
            pl.BlockSpec((None, 1, tq), lambda bi, h, i: (bi, 0, i)),
            pl.BlockSpec((None, s, 1), lambda bi, h, i: (bi, 0, 0)),
            vec(DIFF_QK_DIM), vec(DIFF_QK_DIM), vec(DIFF_QK_DIM), vec(DIFF_QK_DIM),
            vec(DIFF_V_DIM),
        ],
        out_specs=pl.BlockSpec((None, tq, hp * DIFF_V_DIM), lambda bi, h, i: (bi, i, h)),
        out_shape=jax.ShapeDtypeStruct((b, s, DIFF_V_COLS), BF16),
        scratch_shapes=[pltpu.VMEM((1, tq), F32), pltpu.VMEM((rows, tq), F32)] * (2 * hp)
        + [pltpu.VMEM((tk, ATTN_GW), F32)] * SCORE_RING,
        compiler_params=pltpu.CompilerParams(
            dimension_semantics=("parallel", "parallel", "arbitrary"),
            vmem_limit_bytes=VMEM_LIMIT_BYTES),
        name="diff_attn",
    )(slopes, dqt, dk, dvt, posq, posk, lq1, lk1, lq2, lk2, sg)


def _merge_ln_kernel(h_ref, om_ref, od_ref, wgm_ref, wgd_ref, wbm_ref, wbd_ref, wo_ref,
                     g_ref, b_ref, *rest, tn):
    n_cast = (len(rest) - 2) // 2
    o_ref, y_ref = rest[n_cast], rest[-1]
    _cast_slabs(rest[:n_cast] + rest[n_cast + 1:-1])
    h = h_ref[...]
    hb = h.astype(BF16)
    om = om_ref[...]
    od = od_ref[...]
    for c in range(D_MODEL // tn):
        cols = slice(c * tn, (c + 1) * tn)
        gm = jax.nn.sigmoid(_dot(hb, wgm_ref[:, cols]))
        gd = jax.nn.sigmoid(_dot(hb, wgd_ref[:, cols]))
        y = gm * _dot(om, wbm_ref[:, cols]) + gd * _dot(od, wbd_ref[:, cols])
        y_ref[:, cols] = y.astype(BF16)
    mix = _dot(y_ref[...], wo_ref[...])
    o_ref[...] = _layer_norm_rows(ALPHA * h + mix, g_ref[...], b_ref[...])


def _merge_ln(h, om, od, wgm, wgd, wbm, wbd, wo, g, b, cast_weights, *, tm=256, tn=512):
    t, d = h.shape
    steps = t // tm
    row = lambda n: pl.BlockSpec((tm, n), lambda i: (i, 0))
    cast_specs = [_cast_slab_spec(w, steps) for w in cast_weights]
    outs = pl.pallas_call(
        functools.partial(_merge_ln_kernel, tn=tn),
        grid=(steps,),
        in_specs=[row(d), row(om.shape[1]), row(od.shape[1]),
                  _resident(wgm.shape), _resident(wgd.shape), _resident(wbm.shape),
                  _resident(wbd.shape), _resident(wo.shape),
                  _resident(g.shape), _resident(b.shape)] + cast_specs,
        out_specs=[row(d)] + cast_specs,
        out_shape=[jax.ShapeDtypeStruct((t, d), F32)]
        + [jax.ShapeDtypeStruct(w.shape, BF16) for w in cast_weights],
        scratch_shapes=[pltpu.VMEM((tm, d), BF16)],
        compiler_params=pltpu.CompilerParams(
            dimension_semantics=("arbitrary",), vmem_limit_bytes=VMEM_LIMIT_BYTES),
        name="merge_ln",
    )(h, om, od, wgm, wgd, wbm, wbd, wo, g, b, *cast_weights)
    return outs[0], outs[1:]


W_IN_PREP_STEPS = 8
W_IN_N_TRANSPOSED = 4


def _w_in_prep_kernel(*refs):
    n = len(refs) // 2
    for k, (src, dst) in enumerate(zip(refs[:n], refs[n:])):
        w = src[...]
        dst[...] = (w.T if k < W_IN_N_TRANSPOSED else w).astype(BF16)


def _prep_w_in(w_in_t):
    d = w_in_t.shape[1]
    steps = W_IN_PREP_STEPS
    base = MLA_Q_RANK + MLA_KV_RANK
    d0 = base + MLA_ROPE_DIM
    v0 = d0 + 2 * DIFF_QK_COLS
    g0 = v0 + DIFF_V_COLS
    pieces = [(0, base), (d0 + DIFF_QK_COLS, DIFF_QK_COLS), (g0, D_MODEL), (g0 + D_MODEL, D_MODEL),
              (d0, DIFF_QK_COLS), (v0, DIFF_V_COLS)]
    in_specs, out_specs, out_shapes = [], [], []
    for k, (r0, rows) in enumerate(pieces):
        per = rows // steps
        in_specs.append(pl.BlockSpec((pl.Element(per), pl.Element(d)),
                                     lambda i, r0=r0, per=per: (pl.multiple_of(r0 + i * per, 64), 0)))
        if k < W_IN_N_TRANSPOSED:
            out_specs.append(pl.BlockSpec((d, per), lambda i: (0, i)))
            out_shapes.append(jax.ShapeDtypeStruct((d, rows), BF16))
        else:
            out_specs.append(pl.BlockSpec((per, d), lambda i: (i, 0)))
            out_shapes.append(jax.ShapeDtypeStruct((rows, d), BF16))
    in_specs.append(pl.BlockSpec((MLA_ROPE_DIM, d), lambda i: (base // MLA_ROPE_DIM, 0)))
    out_specs.append(pl.BlockSpec((MLA_ROPE_DIM, d), lambda i: (0, 0)))
    out_shapes.append(jax.ShapeDtypeStruct((MLA_ROPE_DIM, d), BF16))
    return pl.pallas_call(
        _w_in_prep_kernel,
        grid=(steps,),
        in_specs=in_specs,
        out_specs=out_specs,
        out_shape=out_shapes,
        compiler_params=pltpu.CompilerParams(
            dimension_semantics=("arbitrary",), vmem_limit_bytes=VMEM_LIMIT_BYTES),
        name="w_in_prep",
    )(*([w_in_t] * len(in_specs)))


def _prep_w_uq_t(w_uq):
    r = w_uq.shape[0]
    wq3 = w_uq.reshape(r, MLA_HEADS, MLA_QK_DIM)
    z3 = jnp.zeros((r, MLA_HEADS, MLA_QK_PAD - MLA_QK_DIM), w_uq.dtype)
    return jnp.concatenate([wq3, z3], axis=-1).reshape(r, MLA_HEADS * MLA_QK_PAD).T


def kernel(x, positions, ln1_g, ln1_b, ffn1_w_gate, ffn1_w_up, ffn1_w_down, w_in, mla_q_norm_g, mla_w_uq, mla_kv_norm_g, mla_w_uk, mla_w_uv, diff_lambda_q1, diff_lambda_k1, diff_lambda_q2, diff_lambda_k2, diff_subln_g, w_branch_mla, w_branch_diff, w_out, ln2_g, ln2_b, ffn2_w_gate, ffn2_w_up, ffn2_w_down, ln3_g, ln3_b):
    b, s, d = x.shape
    t = b * s
    bf = lambda w: w.astype(BF16)

    half = MLA_ROPE_DIM // 2
    inv_freq = ROPE_THETA ** (-jnp.arange(half, dtype=F32) / half)
    invf = inv_freq[:, None]
    slopes = 2.0 ** (-8.0 * jnp.arange(1, DIFF_HEADS + 1, dtype=F32) / DIFF_HEADS)

    h = x.reshape(t, d)
    for l in range(DEPTH):
        lambda_init = 0.8 - 0.6 * math.exp(-0.3 * l)
        head, w1g, w1u, w1d = _ffn_ln_head(h, ffn1_w_gate[l], ffn1_w_up[l], ffn1_w_down[l],
                                           ln1_g[l][None, :], ln1_b[l][None, :])
        h = _ffn_ln(h, w1g, w1u, w1d, ln1_g[l][None, :], ln1_b[l][None, :], head_tile=head)

        wlat, wdk, wgm, wgd, wdqt, wdvt, wkrt = _prep_w_in(w_in[l].T)
        wqt = _prep_w_uq_t(bf(mla_w_uq[l]))
        pieces = s // PROJ_TM
        qt, k, vt, dqt, dk, dvt = _mixer_proj(
            h, positions.reshape(t // PROJ_TM, 1, PROJ_TM).astype(F32), invf,
            wlat, wkrt, wdqt, wdk, wdvt, mla_q_norm_g[l][None, :], mla_kv_norm_g[l][None, :],
            wqt, bf(mla_w_uk[l]), bf(mla_w_uv[l]).T)

        o_mla, (wbm, wbd, wo) = _mla_attn(
            qt.reshape((b, pieces) + qt.shape[1:]), k.reshape(b, s, -1),
            vt.reshape((b, pieces) + vt.shape[1:]),
            (w_branch_mla[l], w_branch_diff[l], w_out[l]))
        o_diff = _diff_attn(
            slopes, dqt.reshape(b, pieces, DIFF_HEADS, 2, DIFF_QK_DIM, PROJ_TM),
            dk.reshape(b, s, -1), dvt.reshape((b, pieces) + dvt.shape[1:]), positions,
            diff_lambda_q1[l][None, :], diff_lambda_k1[l][None, :],
            diff_lambda_q2[l][None, :], diff_lambda_k2[l][None, :],
            diff_subln_g[l][None, :], lambda_init=lambda_init)

        h, (w2g, w2u, w2d) = _merge_ln(
            h, o_mla.reshape(t, -1), o_diff.reshape(t, -1), wgm, wgd, wbm, wbd, wo,
            ln2_g[l][None, :], ln2_b[l][None, :],
            (ffn2_w_gate[l], ffn2_w_up[l], ffn2_w_down[l]))

        h = _ffn_ln(h, w2g, w2u, w2d, ln3_g[l][None, :], ln3_b[l][None, :])
    return h.reshape(b, s, d)
```

```python
import functools
import math

import jax
import jax.numpy as jnp
from jax import lax
from jax.experimental import pallas as pl
from jax.experimental.pallas import tpu as pltpu

D_MODEL = 2048
DEPTH = 1
MLA_HEADS = 8
MLA_Q_RANK = 512
MLA_KV_RANK = 512
MLA_NOPE_DIM = 128
MLA_ROPE_DIM = 64
MLA_V_DIM = 128
MLA_QK_DIM = MLA_NOPE_DIM + MLA_ROPE_DIM
MLA_QK_PAD = 256
ROPE_THETA = 10000.0
DIFF_HEADS = 4
DIFF_QK_DIM = 128
DIFF_V_DIM = 2 * DIFF_QK_DIM
DIFF_QK_COLS = DIFF_HEADS * 2 * DIFF_QK_DIM
DIFF_V_COLS = DIFF_HEADS * DIFF_V_DIM
LN_EPS = 1e-5
RMS_EPS = 1e-6
ALPHA = (2 * DEPTH) ** 0.25
LOG2E = math.log2(math.e)

LANE = 128
MXU_DIM = 256
ONES_ROWS = 16
VMEM_LIMIT_BYTES = 56 * 1024 * 1024
VMEM_INTERNAL_SCRATCH_BYTES = 8 * 1024 * 1024
PROJ_TM = MXU_DIM
ATTN_TQ = 2048
ATTN_TK = 512
ATTN_GW = 2 * MXU_DIM
GROUP_PIECES = ATTN_GW // PROJ_TM
LOOKAHEAD = 2
SCORE_RING = 4
MLA_HEADS_PER_STEP = 4
DIFF_HEADS_PER_STEP = 2

F32 = jnp.float32
BF16 = jnp.bfloat16


def _dot(a, b):
    return jnp.dot(a, b, preferred_element_type=F32)


def _dot_nt(a, b):
    return lax.dot_general(a, b, (((1,), (1,)), ((), ())), preferred_element_type=F32)


def _layer_norm_rows(y, g, b):
    mu = jnp.mean(y, axis=-1, keepdims=True)
    d = y - mu
    var = jnp.mean(d * d, axis=-1, keepdims=True)
    return d * lax.rsqrt(var + LN_EPS) * g + b


def _rms_norm_rows(y, g):
    ms = jnp.mean(y * y, axis=-1, keepdims=True)
    return y * lax.rsqrt(ms + RMS_EPS) * g


def _resident(shape):
    return pl.BlockSpec(shape, lambda *_: (0,) * len(shape), pipeline_mode=pl.Buffered(1))


def _cast_slab_spec(w, n_steps, step_of=lambda *idx: idx[0]):
    rows, cols = w.shape
    slabs = n_steps
    while rows % slabs or (rows // slabs) % 16:
        slabs //= 2
    per = n_steps // slabs
    return pl.BlockSpec((rows // slabs, cols), lambda *idx: (step_of(*idx) // per, 0))


def _cast_slabs(refs):
    n = len(refs) // 2
    for src, dst in zip(refs[:n], refs[n:]):
        dst[...] = src[...].astype(BF16)


def _ffn_step(x_ref, wg_ref, wu_ref, wd_ref, g_ref, b_ref, o_ref, xb_ref):
    j = pl.program_id(1)

    @pl.when(j == 0)
    def _():
        x = x_ref[...]
        xb_ref[...] = x.astype(BF16)
        o_ref[...] = ALPHA * x

    xb = xb_ref[...]
    gate = _dot(xb, wg_ref[...])
    up = _dot(xb, wu_ref[...])
    act = (0.5 * (gate * jax.nn.sigmoid(gate))) * up
    o_ref[...] += _dot(act.astype(BF16), wd_ref[...])

    @pl.when(j == pl.num_programs(1) - 1)
    def _():
        o_ref[...] = _layer_norm_rows(o_ref[...], g_ref[...], b_ref[...])


def _ffn_ln_kernel(x_ref, wg_ref, wu_ref, wd_ref, g_ref, b_ref, o_ref, xb_ref):
    _ffn_step(x_ref, wg_ref, wu_ref, wd_ref, g_ref, b_ref, o_ref, xb_ref)


def _ffn_ln_tail_kernel(x_ref, wg_ref, wu_ref, wd_ref, g_ref, b_ref, head_ref,
                        o_ref, xb_ref, sem):
    i = pl.program_id(0)

    @pl.when(jnp.logical_and(i == 0, pl.program_id(1) == 0))
    def _():
        copy = pltpu.make_async_copy(head_ref, o_ref, sem)
        copy.start()
        copy.wait()

    @pl.when(i > 0)
    def _():
        _ffn_step(x_ref, wg_ref, wu_ref, wd_ref, g_ref, b_ref, o_ref, xb_ref)


def _ffn_ln_head_kernel(x_ref, wg_ref, wu_ref, wd_ref, g_ref, b_ref,
                        o_ref, wgb_ref, wub_ref, wdb_ref, xb_ref):
    _cast_slabs((wg_ref, wu_ref, wd_ref, wgb_ref, wub_ref, wdb_ref))
    _ffn_step(x_ref, wgb_ref, wub_ref, wdb_ref, g_ref, b_ref, o_ref, xb_ref)


def _ffn_specs(tm, tf, d, skip_first_tile=False):
    col = (lambda i, j: jnp.where(i == 0, 0, j)) if skip_first_tile else (lambda i, j: j)
    return [
        pl.BlockSpec((tm, d), lambda i, j: (i, 0)),
        pl.BlockSpec((d, tf), lambda i, j: (0, col(i, j))),
        pl.BlockSpec((d, tf), lambda i, j: (0, col(i, j))),
        pl.BlockSpec((tf, d), lambda i, j: (col(i, j), 0)),
        pl.BlockSpec((1, d), lambda i, j: (0, 0)),
        pl.BlockSpec((1, d), lambda i, j: (0, 0)),
    ]


def _ffn_ln_head(x, wg, wu, wd, g, b, *, tm=1024, tf=256):
    d = x.shape[1]
    f = wg.shape[1]
    specs = _ffn_specs(tm, tf, d)
    vmem_bytes = ((2 * tm * d * 4) * 2 + tm * d * 2 + 2 * (3 * d * tf * (4 + 2))
                  + tm * tf * (4 + 4 + 2))
    return pl.pallas_call(
        _ffn_ln_head_kernel,
        grid=(1, f // tf),
        in_specs=specs,
        out_specs=[specs[0]] + specs[1:4],
        out_shape=[jax.ShapeDtypeStruct((tm, d), F32)]
        + [jax.ShapeDtypeStruct(w.shape, BF16) for w in (wg, wu, wd)],
        scratch_shapes=[pltpu.VMEM((tm, d), BF16)],
        compiler_params=pltpu.CompilerParams(
            dimension_semantics=("arbitrary", "arbitrary"),
            vmem_limit_bytes=vmem_bytes + VMEM_INTERNAL_SCRATCH_BYTES),
        name="ffn_ln_head",
    )(x, wg, wu, wd, g, b)


def _ffn_ln(x, wg, wu, wd, g, b, *, head_tile=None, tm=1024, tf=512):
    t, d = x.shape
    f = wg.shape[1]
    with_head = head_tile is not None
    specs = _ffn_specs(tm, tf, d, skip_first_tile=with_head)
    vmem_bytes = (2 * tm * d * 4) * 2 + tm * d * 2 + 2 * (3 * d * tf * 2) + tm * tf * (4 + 4 + 2)
    return pl.pallas_call(
        _ffn_ln_tail_kernel if with_head else _ffn_ln_kernel,
        grid=(t // tm, f // tf),
        in_specs=specs + ([pl.BlockSpec(memory_space=pl.ANY)] if with_head else []),
        out_specs=specs[0],
        out_shape=jax.ShapeDtypeStruct((t, d), F32),
        scratch_shapes=[pltpu.VMEM((tm, d), BF16)]
        + ([pltpu.SemaphoreType.DMA(())] if with_head else []),
        compiler_params=pltpu.CompilerParams(
            dimension_semantics=("arbitrary", "arbitrary"),
            vmem_limit_bytes=vmem_bytes + VMEM_INTERNAL_SCRATCH_BYTES),
        name="ffn_ln",
    )(x, wg, wu, wd, g, b, *((head_tile,) if with_head else ()))


def _mixer_proj_kernel(h_ref, pos_ref, invf_ref, wlat_ref, wkrt_ref, wdqt_ref, wdk_ref, wdvt_ref,
                       qg_ref, kvg_ref, wqt_ref, wuk_ref, wuvt_ref,
                       qt_ref, k_ref, vt_ref, dqt_ref, dk_ref, dvt_ref):
    hb = h_ref[...].astype(BF16)
    tm = hb.shape[0]

    dq_scale = DIFF_QK_DIM ** -0.5 * LOG2E
    dqt_ref[...] = (_dot_nt(wdqt_ref[...], hb) * dq_scale).astype(BF16)
    dk_ref[...] = _dot(hb, wdk_ref[...]).astype(BF16)
    ones_rows = (lax.broadcasted_iota(jnp.int32, (ONES_ROWS, tm), 0) == 0).astype(BF16)
    dvt = _dot_nt(wdvt_ref[...], hb).astype(BF16)
    for h in range(DIFF_HEADS):
        dvt_ref[h, 0:DIFF_V_DIM, :] = dvt[h * DIFF_V_DIM:(h + 1) * DIFF_V_DIM, :]
        dvt_ref[h, DIFF_V_DIM:, :] = ones_rows

    lat = _dot(hb, wlat_ref[...])
    c_q = lat[:, 0:MLA_Q_RANK]
    c_kv = lat[:, MLA_Q_RANK:MLA_Q_RANK + MLA_KV_RANK]
    k_r_t = _dot_nt(wkrt_ref[...], hb)

    ang = invf_ref[...] * pos_ref[...]
    cos_f, sin_f = jnp.cos(ang), jnp.sin(ang)
    half = MLA_ROPE_DIM // 2
    pad = jnp.zeros((LANE - MLA_ROPE_DIM, tm), F32)

    def rope_t(r):
        x1, x2 = r[0:half, :], r[half:2 * half, :]
        return jnp.concatenate([x1 * cos_f - x2 * sin_f, x1 * sin_f + x2 * cos_f, pad], axis=0)

    cqn = _rms_norm_rows(c_q, qg_ref[...]).astype(BF16)
    ckvn = _rms_norm_rows(c_kv, kvg_ref[...]).astype(BF16)

    q_t = _dot_nt(wqt_ref[...], cqn)
    k_nope = _dot(ckvn, wuk_ref[...])
    vt = _dot_nt(wuvt_ref[...], ckvn).astype(BF16)
    for h in range(MLA_HEADS):
        vt_ref[h, 0:MLA_V_DIM, :] = vt[h * MLA_V_DIM:(h + 1) * MLA_V_DIM, :]
        vt_ref[h, MLA_V_DIM:, :] = ones_rows

    q_scale = MLA_QK_DIM ** -0.5 * LOG2E
    k_rope = rope_t(k_r_t).T.astype(BF16)
    for h in range(MLA_HEADS):
        c0 = h * MLA_QK_PAD
        qt_ref[h, 0:LANE, :] = (q_t[c0:c0 + LANE, :] * q_scale).astype(BF16)
        q_rope = rope_t(q_t[c0 + LANE:c0 + LANE + MLA_ROPE_DIM, :])
        qt_ref[h, LANE:2 * LANE, :] = (q_rope * q_scale).astype(BF16)
        k_ref[:, c0:c0 + LANE] = k_nope[:, h * LANE:(h + 1) * LANE].astype(BF16)
        k_ref[:, c0 + LANE:c0 + 2 * LANE] = k_rope


def _mixer_proj(h, pos, invf, wlat, wkrt, wdqt, wdk, wdvt, qg, kvg, wqt, wuk, wuvt):
    t, d = h.shape
    tm = PROJ_TM
    qk_cols = MLA_HEADS * MLA_QK_PAD
    row = lambda n: pl.BlockSpec((tm, n), lambda i: (i, 0))
    tile_t = lambda *dims: pl.BlockSpec((None,) + dims + (tm,),
                                        lambda i: (i,) + (0,) * (len(dims) + 1))
    rows_out = lambda n: jax.ShapeDtypeStruct((t, n), BF16)
    tile_t_out = lambda *dims: jax.ShapeDtypeStruct((t // tm,) + dims + (tm,), BF16)
    weights = (invf, wlat, wkrt, wdqt, wdk, wdvt, qg, kvg, wqt, wuk, wuvt)
    return pl.pallas_call(
        _mixer_proj_kernel,
        grid=(t // tm,),
        in_specs=[row(d), tile_t(1)] + [_resident(w.shape) for w in weights],
        out_specs=[tile_t(MLA_HEADS, MLA_QK_PAD), row(qk_cols),
                   tile_t(MLA_HEADS, MLA_V_DIM + ONES_ROWS),
                   tile_t(DIFF_QK_COLS), row(DIFF_QK_COLS),
                   tile_t(DIFF_HEADS, DIFF_V_DIM + ONES_ROWS)],
        out_shape=[tile_t_out(MLA_HEADS, MLA_QK_PAD), rows_out(qk_cols),
                   tile_t_out(MLA_HEADS, MLA_V_DIM + ONES_ROWS),
                   tile_t_out(DIFF_QK_COLS), rows_out(DIFF_QK_COLS),
                   tile_t_out(DIFF_HEADS, DIFF_V_DIM + ONES_ROWS)],
        compiler_params=pltpu.CompilerParams(
            dimension_semantics=("parallel",), vmem_limit_bytes=VMEM_LIMIT_BYTES),
        name="mixer_proj",
    )(h, pos, *weights)


def _diag_units(tq, tk):
    units = []
    for j in range(tq // tk):
        for g in range(tq // ATTN_GW):
            if j * tk >= (g + 1) * ATTN_GW:
                continue
            visible = (j + 1) * tk <= g * ATTN_GW + 1
            units.append((j, g, None if visible else j * tk))
    return units


def _causal_mask_t(s, g, key_off):
    rows = lax.broadcasted_iota(jnp.int32, s.shape, 0)
    cols = lax.broadcasted_iota(jnp.int32, s.shape, 1)
    return jnp.where(rows + key_off <= cols + g * ATTN_GW, s, -jnp.inf)


def _run_pipelined(units, s_refs, score_fn, consume_fn, next_units=()):
    ring = len(s_refs)
    todo = list(enumerate(list(units) + list(next_units)))[LOOKAHEAD:]
    for k, u in enumerate(units):
        if todo:
            kk, nu = todo.pop(0)
            score_fn(nu, s_refs[kk % ring])
        consume_fn(u, s_refs[k % ring][...])


def _prime_pipeline(units, s_refs, score_fn):
    for k, u in enumerate(units[:LOOKAHEAD]):
        score_fn(u, s_refs[k])


def _softmax_step_t(s, vt_pieces, m_ref, acc_ref, cols):
    m_prev = m_ref[:, cols]
    m_new = jnp.maximum(m_prev, jnp.max(s, axis=0, keepdims=True))
    corr = jnp.exp2(m_prev - m_new)
    p = jnp.exp2(s - m_new).astype(BF16)
    pv = None
    for u, vt in enumerate(vt_pieces):
        part = _dot(vt, p[u * PROJ_TM:(u + 1) * PROJ_TM, :])
        pv = part if pv is None else pv + part
    acc_ref[:, cols] = corr * acc_ref[:, cols] + pv
    m_ref[:, cols] = m_new


def _init_stats(m_ref, acc_ref):
    m_ref[...] = jnp.full(m_ref.shape, -jnp.inf, F32)
    acc_ref[...] = jnp.zeros(acc_ref.shape, F32)


def _normalized(acc_ref, dv):
    return acc_ref[0:dv, :] * (1.0 / acc_ref[dv:dv + 1, :])


def _mla_attn_kernel(qt_ref, k_ref, vt_ref, *rest, tq, tk, n_cast):
    cast_in, o_ref, cast_out = rest[:n_cast], rest[n_cast], rest[n_cast + 1:2 * n_cast + 1]
    scratch = rest[2 * n_cast + 1:]
    n_heads = MLA_HEADS_PER_STEP
    stats = [scratch[2 * hh:2 * hh + 2] for hh in range(n_heads)]
    s_refs = scratch[2 * n_heads:]
    _cast_slabs(cast_in + cast_out)
    qi = pl.program_id(2)
    n_piece = tk // PROJ_TM
    n_group = tq // ATTN_GW
    n_full = qi * (tq // tk)
    for m_ref, acc_ref in stats:
        _init_stats(m_ref, acc_ref)

    def score(unit, s_ref):
        hh, c, g, _ = unit
        kc = k_ref[pl.ds(pl.multiple_of(c * tk, tk), tk), hh * MLA_QK_PAD:(hh + 1) * MLA_QK_PAD]
        for half in range(GROUP_PIECES):
            s_ref[:, half * PROJ_TM:(half + 1) * PROJ_TM] = _dot(
                kc, qt_ref[g * GROUP_PIECES + half, hh])

    def consume(unit, s):
        hh, c, g, key_off = unit
        if key_off is not None:
            s = _causal_mask_t(s, g, key_off)
        m_ref, acc_ref = stats[hh]
        vts = [vt_ref[c * n_piece + u, hh] for u in range(n_piece)]
        _softmax_step_t(s, vts, m_ref, acc_ref, slice(g * ATTN_GW, (g + 1) * ATTN_GW))

    def chunk_units(c):
        return [(hh, c, g, None) for g in range(n_group) for hh in range(n_heads)]

    assert (n_group * n_heads) % len(s_refs) == 0

    def body(c, carry):
        _run_pipelined(chunk_units(c), s_refs, score, consume, next_units=chunk_units(c + 1))
        return carry

    _prime_pipeline(chunk_units(0), s_refs, score)
    lax.fori_loop(0, n_full, body, 0)
    diag = [(hh, n_full + j, g, off) for j, g, off in _diag_units(tq, tk)
            for hh in range(n_heads)]
    _run_pipelined(diag, s_refs, score, consume)
    for hh, (_, acc_ref) in enumerate(stats):
        o_ref[:, hh * MLA_V_DIM:(hh + 1) * MLA_V_DIM] = (
            _normalized(acc_ref, MLA_V_DIM).T.astype(o_ref.dtype))


def _mla_attn(qt, k, vt, cast_weights):
    b, s, _ = k.shape
    tq, tk = ATTN_TQ, ATTN_TK
    rows = MLA_V_DIM + ONES_ROWS
    nq = s // tq
    hp = MLA_HEADS_PER_STEP
    n_hp = MLA_HEADS // hp
    steps = b * n_hp * nq
    cast_specs = [_cast_slab_spec(w, steps, lambda bi, h, i: (bi * n_hp + h) * nq + i)
                  for w in cast_weights]
    outs = pl.pallas_call(
        functools.partial(_mla_attn_kernel, tq=tq, tk=tk, n_cast=len(cast_weights)),
        grid=(b, n_hp, nq),
        in_specs=[
            pl.BlockSpec((None, tq // PROJ_TM, hp, MLA_QK_PAD, PROJ_TM),
                         lambda bi, h, i: (bi, i, h, 0, 0)),
            pl.BlockSpec((None, s, hp * MLA_QK_PAD), lambda bi, h, i: (bi, 0, h)),
            pl.BlockSpec((None, s // PROJ_TM, hp, rows, PROJ_TM),
                         lambda bi, h, i: (bi, 0, h, 0, 0)),
        ] + cast_specs,
        out_specs=[pl.BlockSpec((None, tq, hp * MLA_V_DIM), lambda bi, h, i: (bi, i, h))]
        + cast_specs,
        out_shape=[jax.ShapeDtypeStruct((b, s, MLA_HEADS * MLA_V_DIM), BF16)]
        + [jax.ShapeDtypeStruct(w.shape, BF16) for w in cast_weights],
        scratch_shapes=[pltpu.VMEM((1, tq), F32), pltpu.VMEM((rows, tq), F32)] * hp
        + [pltpu.VMEM((tk, ATTN_GW), F32)] * SCORE_RING,
        compiler_params=pltpu.CompilerParams(
            dimension_semantics=("arbitrary", "arbitrary", "arbitrary"),
            vmem_limit_bytes=VMEM_LIMIT_BYTES),
        name="mla_attn",
    )(qt, k, vt, *cast_weights)
    return outs[0], outs[1:]


def _diff_attn_kernel(slopes_ref, qt_ref, k_ref, vt_ref, posq_ref, posk_ref,
                      lq1_ref, lk1_ref, lq2_ref, lk2_ref, sg_ref, o_ref, *scratch,
                      tq, tk, lambda_init):
    hp = pl.program_id(1)
    qi = pl.program_id(2)
    n_piece = tk // PROJ_TM
    n_group = tq // ATTN_GW
    n_full = qi * (tq // tk)
    n_heads = DIFF_HEADS_PER_STEP
    n_maps = 2 * n_heads
    stats = [[scratch[2 * (2 * hh + mi):2 * (2 * hh + mi) + 2] for mi in range(2)]
             for hh in range(n_heads)]
    s_refs = scratch[2 * n_maps:]
    slopes = [slopes_ref[hp * n_heads + hh] * LOG2E for hh in range(n_heads)]
    for per_head in stats:
        for m_ref, acc_ref in per_head:
            _init_stats(m_ref, acc_ref)
    hd = 2 * DIFF_QK_DIM

    def make_fns():
        dist_cache, bias_cache = {}, {}

        def score(unit, s_ref):
            ckey, c, g, hh, mi, _ = unit
            rows = pl.ds(pl.multiple_of(c * tk, tk), tk)
            if (ckey, g) not in dist_cache:
                posk = posk_ref[rows, :]
                posq = posq_ref[:, g * ATTN_GW:(g + 1) * ATTN_GW]
                dist_cache[(ckey, g)] = jnp.abs(posk - posq)
            if (ckey, g, hh) not in bias_cache:
                bias_cache[(ckey, g, hh)] = slopes[hh] * dist_cache[(ckey, g)]
            bias = bias_cache[(ckey, g, hh)]
            col0 = hh * hd + mi * DIFF_QK_DIM
            kc = k_ref[rows, col0:col0 + DIFF_QK_DIM]
            for half in range(GROUP_PIECES):
                lanes = slice(half * PROJ_TM, (half + 1) * PROJ_TM)
                s_ref[:, lanes] = (_dot(kc, qt_ref[g * GROUP_PIECES + half, hh, mi])
                                   - bias[:, lanes])

        def consume(unit, s):
            _, c, g, hh, mi, key_off = unit
            if key_off is not None:
                s = _causal_mask_t(s, g, key_off)
            m_ref, acc_ref = stats[hh][mi]
            vts = [vt_ref[c * n_piece + u, hh] for u in range(n_piece)]
            _softmax_step_t(s, vts, m_ref, acc_ref, slice(g * ATTN_GW, (g + 1) * ATTN_GW))

        return score, consume

    def chunk_units(ckey, c):
        return [(ckey, c, g, hh, mi, None) for g in range(n_group)
                for hh in range(n_heads) for mi in range(2)]

    assert (n_group * n_maps) % len(s_refs) == 0

    def body(c, carry):
        score, consume = make_fns()
        _run_pipelined(chunk_units("cur", c), s_refs, score, consume,
                       next_units=chunk_units("next", c + 1))
        return carry

    score, consume = make_fns()
    _prime_pipeline(chunk_units("first", 0), s_refs, score)
    lax.fori_loop(0, n_full, body, 0)
    diag = [(j, n_full + j, g, hh, mi, off) for j, g, off in _diag_units(tq, tk)
            for hh in range(n_heads) for mi in range(2)]
    _run_pipelined(diag, s_refs, score, consume)

    lam = (jnp.exp(jnp.sum(lq1_ref[...] * lk1_ref[...], axis=-1, keepdims=True))
           - jnp.exp(jnp.sum(lq2_ref[...] * lk2_ref[...], axis=-1, keepdims=True))
           + lambda_init)
    for hh, ((_, acc0_ref), (_, acc1_ref)) in enumerate(stats):
        o_t = _normalized(acc0_ref, DIFF_V_DIM) - lam * _normalized(acc1_ref, DIFF_V_DIM)
        o = _rms_norm_rows(o_t.T, sg_ref[...]) * (1.0 - lambda_init)
        o_ref[:, hh * DIFF_V_DIM:(hh + 1) * DIFF_V_DIM] = o.astype(o_ref.dtype)


def _diff_attn(slopes, dqt, dk, dvt, positions, lq1, lk1, lq2, lk2, sg, *, lambda_init):
    b, s, _ = dk.shape
    tq, tk = ATTN_TQ, ATTN_TK
    hp = DIFF_HEADS_PER_STEP
    hd = 2 * DIFF_QK_DIM
    rows = DIFF_V_DIM + ONES_ROWS
    posf = positions.astype(F32)
    posq = posf.reshape(b, 1, s)
    posk = posf.reshape(b, s, 1)
    vec = lambda n: pl.BlockSpec((1, n), lambda bi, h, i: (0, 0))
    return pl.pallas_call(
        functools.partial(_diff_attn_kernel, tq=tq, tk=tk, lambda_init=lambda_init),
        grid=(b, DIFF_HEADS // hp, s // tq),
        in_specs=[
            pl.BlockSpec(memory_space=pltpu.SMEM),
            pl.BlockSpec((None, tq // PROJ_TM, hp, 2, DIFF_QK_DIM, PROJ_TM),
                         lambda bi, h, i: (bi, i, h, 0, 0, 0)),
            pl.BlockSpec((None, s, hp * hd), lambda bi, h, i: (bi, 0, h)),
            pl.BlockSpec((None, s // PROJ_TM, hp, rows, PROJ_TM),
                         lambda bi, h, i: (bi, 0, h, 0, 0)),
            pl.BlockSpec((None, 1, tq), lambda bi, h, i: (bi, 0, i)),
            pl.BlockSpec((None, s, 1), lambda bi, h, i: (bi, 0, 0)),
            vec(DIFF_QK_DIM), vec(DIFF_QK_DIM), vec(DIFF_QK_DIM), vec(DIFF_QK_DIM),
            vec(DIFF_V_DIM),
        ],
        out_specs=pl.BlockSpec((None, tq, hp * DIFF_V_DIM), lambda bi, h, i: (bi, i, h)),
        out_shape=jax.ShapeDtypeStruct((b, s, DIFF_V_COLS), BF16),
        scratch_shapes=[pltpu.VMEM((1, tq), F32), pltpu.VMEM((rows, tq), F32)] * (2 * hp)
        + [pltpu.VMEM((tk, ATTN_GW), F32)] * SCORE_RING,
        compiler_params=pltpu.CompilerParams(
            dimension_semantics=("parallel", "parallel", "arbitrary"),
            vmem_limit_bytes=VMEM_LIMIT_BYTES),
        name="diff_attn",
    )(slopes, dqt, dk, dvt, posq, posk, lq1, lk1, lq2, lk2, sg)


def _merge_ln_kernel(h_ref, om_ref, od_ref, wgm_ref, wgd_ref, wbm_ref, wbd_ref, wo_ref,
                     g_ref, b_ref, *rest, tn):
    n_cast = (len(rest) - 2) // 2
    o_ref, y_ref = rest[n_cast], rest[-1]
    _cast_slabs(rest[:n_cast] + rest[n_cast + 1:-1])
    h = h_ref[...]
    hb = h.astype(BF16)
    om = om_ref[...]
    od = od_ref[...]
    for c in range(D_MODEL // tn):
        cols = slice(c * tn, (c + 1) * tn)
        gm = jax.nn.sigmoid(_dot(hb, wgm_ref[:, cols]))
        gd = jax.nn.sigmoid(_dot(hb, wgd_ref[:, cols]))
        y = gm * _dot(om, wbm_ref[:, cols]) + gd * _dot(od, wbd_ref[:, cols])
        y_ref[:, cols] = y.astype(BF16)
    mix = _dot(y_ref[...], wo_ref[...])
    o_ref[...] = _layer_norm_rows(ALPHA * h + mix, g_ref[...], b_ref[...])


def _merge_ln(h, om, od, wgm, wgd, wbm, wbd, wo, g, b, cast_weights, *, tm=256, tn=512):
    t, d = h.shape
    steps = t // tm
    row = lambda n: pl.BlockSpec((tm, n), lambda i: (i, 0))
    cast_specs = [_cast_slab_spec(w, steps) for w in cast_weights]
    outs = pl.pallas_call(
        functools.partial(_merge_ln_kernel, tn=tn),
        grid=(steps,),
        in_specs=[row(d), row(om.shape[1]), row(od.shape[1]),
                  _resident(wgm.shape), _resident(wgd.shape), _resident(wbm.shape),
                  _resident(wbd.shape), _resident(wo.shape),
                  _resident(g.shape), _resident(b.shape)] + cast_specs,
        out_specs=[row(d)] + cast_specs,
        out_shape=[jax.ShapeDtypeStruct((t, d), F32)]
        + [jax.ShapeDtypeStruct(w.shape, BF16) for w in cast_weights],
        scratch_shapes=[pltpu.VMEM((tm, d), BF16)],
        compiler_params=pltpu.CompilerParams(
            dimension_semantics=("arbitrary",), vmem_limit_bytes=VMEM_LIMIT_BYTES),
        name="merge_ln",
    )(h, om, od, wgm, wgd, wbm, wbd, wo, g, b, *cast_weights)
    return outs[0], outs[1:]


W_IN_PREP_STEPS = 8
W_IN_N_TRANSPOSED = 4


def _w_in_prep_kernel(*refs):
    n = len(refs) // 2
    for k, (src, dst) in enumerate(zip(refs[:n], refs[n:])):
        w = src[...]
        dst[...] = (w.T if k < W_IN_N_TRANSPOSED else w).astype(BF16)


def _prep_w_in(w_in_t):
    d = w_in_t.shape[1]
    steps = W_IN_PREP_STEPS
    base = MLA_Q_RANK + MLA_KV_RANK
    d0 = base + MLA_ROPE_DIM
    v0 = d0 + 2 * DIFF_QK_COLS
    g0 = v0 + DIFF_V_COLS
    pieces = [(0, base), (d0 + DIFF_QK_COLS, DIFF_QK_COLS), (g0, D_MODEL), (g0 + D_MODEL, D_MODEL),
              (d0, DIFF_QK_COLS), (v0, DIFF_V_COLS)]
    in_specs, out_specs, out_shapes = [], [], []
    for k, (r0, rows) in enumerate(pieces):
        per = rows // steps
        in_specs.append(pl.BlockSpec((pl.Element(per), pl.Element(d)),
                                     lambda i, r0=r0, per=per: (pl.multiple_of(r0 + i * per, 64), 0)))
        if k < W_IN_N_TRANSPOSED:
            out_specs.append(pl.BlockSpec((d, per), lambda i: (0, i)))
            out_shapes.append(jax.ShapeDtypeStruct((d, rows), BF16))
        else:
            out_specs.append(pl.BlockSpec((per, d), lambda i: (i, 0)))
            out_shapes.append(jax.ShapeDtypeStruct((rows, d), BF16))
    in_specs.append(pl.BlockSpec((MLA_ROPE_DIM, d), lambda i: (base // MLA_ROPE_DIM, 0)))
    out_specs.append(pl.BlockSpec((MLA_ROPE_DIM, d), lambda i: (0, 0)))
    out_shapes.append(jax.ShapeDtypeStruct((MLA_ROPE_DIM, d), BF16))
    return pl.pallas_call(
        _w_in_prep_kernel,
        grid=(steps,),
        in_specs=in_specs,
        out_specs=out_specs,
        out_shape=out_shapes,
        compiler_params=pltpu.CompilerParams(
            dimension_semantics=("arbitrary",), vmem_limit_bytes=VMEM_LIMIT_BYTES),
        name="w_in_prep",
    )(*([w_in_t] * len(in_specs)))


def _prep_w_uq_t(w_uq):
    r = w_uq.shape[0]
    wq3 = w_uq.reshape(r, MLA_HEADS, MLA_QK_DIM)
    z3 = jnp.zeros((r, MLA_HEADS, MLA_QK_PAD - MLA_QK_DIM), w_uq.dtype)
    return jnp.concatenate([wq3, z3], axis=-1).reshape(r, MLA_HEADS * MLA_QK_PAD).T


def kernel(x, positions, ln1_g, ln1_b, ffn1_w_gate, ffn1_w_up, ffn1_w_down, w_in, mla_q_norm_g, mla_w_uq, mla_kv_norm_g, mla_w_uk, mla_w_uv, diff_lambda_q1, diff_lambda_k1, diff_lambda_q2, diff_lambda_k2, diff_subln_g, w_branch_mla, w_branch_diff, w_out, ln2_g, ln2_b, ffn2_w_gate, ffn2_w_up, ffn2_w_down, ln3_g, ln3_b):
    b, s, d = x.shape
    t = b * s
    bf = lambda w: w.astype(BF16)

    half = MLA_ROPE_DIM // 2
    inv_freq = ROPE_THETA ** (-jnp.arange(half, dtype=F32) / half)
    invf = inv_freq[:, None]
    slopes = 2.0 ** (-8.0 * jnp.arange(1, DIFF_HEADS + 1, dtype=F32) / DIFF_HEADS)

    h = x.reshape(t, d)
    for l in range(DEPTH):
        lambda_init = 0.8 - 0.6 * math.exp(-0.3 * l)
        head, w1g, w1u, w1d = _ffn_ln_head(h, ffn1_w_gate[l], ffn1_w_up[l], ffn1_w_down[l],
                                           ln1_g[l][None, :], ln1_b[l][None, :])
        h = _ffn_ln(h, w1g, w1u, w1d, ln1_g[l][None, :], ln1_b[l][None, :], head_tile=head)

        wlat, wdk, wgm, wgd, wdqt, wdvt, wkrt = _prep_w_in(w_in[l].T)
        wqt = _prep_w_uq_t(bf(mla_w_uq[l]))
        pieces = s // PROJ_TM
        qt, k, vt, dqt, dk, dvt = _mixer_proj(
            h, positions.reshape(t // PROJ_TM, 1, PROJ_TM).astype(F32), invf,
            wlat, wkrt, wdqt, wdk, wdvt, mla_q_norm_g[l][None, :], mla_kv_norm_g[l][None, :],
            wqt, bf(mla_w_uk[l]), bf(mla_w_uv[l]).T)

        o_mla, (wbm, wbd, wo) = _mla_attn(
            qt.reshape((b, pieces) + qt.shape[1:]), k.reshape(b, s, -1),
            vt.reshape((b, pieces) + vt.shape[1:]),
            (w_branch_mla[l], w_branch_diff[l], w_out[l]))
        o_diff = _diff_attn(
            slopes, dqt.reshape(b, pieces, DIFF_HEADS, 2, DIFF_QK_DIM, PROJ_TM),
            dk.reshape(b, s, -1), dvt.reshape((b, pieces) + dvt.shape[1:]), positions,
            diff_lambda_q1[l][None, :], diff_lambda_k1[l][None, :],
            diff_lambda_q2[l][None, :], diff_lambda_k2[l][None, :],
            diff_subln_g[l][None, :], lambda_init=lambda_init)

        h, (w2g, w2u, w2d) = _merge_ln(
            h, o_mla.reshape(t, -1), o_diff.reshape(t, -1), wgm, wgd, wbm, wbd, wo,
            ln2_g[l][None, :], ln2_b[l][None, :],
            (ffn2_w_gate[l], ffn2_w_up[l], ffn2_w_down[l]))

        h = _ffn_ln(h, w2g, w2u, w2d, ln3_g[l][None, :], ln3_b[l][None, :])
    return h.reshape(b, s, d)
```
